```python
import math
import jax, jax.numpy as jnp
from jax import lax
import numpy as np

D_MODEL = 1024
BATCH = 8
SEQ = 2048
DEPTH = 1
DEC_BATCH = 1
DEC_SEQ = 16384
PAST_LEN = 128

D_RNN = 1024
RNN_BLOCKS = 16
RNN_BLOCK = D_RNN // RNN_BLOCKS
RNN_CONV = 4
RNN_CONV_LEFT = 2
LRU_C = 8.0
N_HEADS = 16
QK_NOPE = 64
QK_ROPE = 32
V_HEAD = 64
Q_LORA = 384
KV_LORA = 256
ROPE_BASE = 10000.0
Q_BLOCK = 128
D_FF = 2816
FFN_CONV = 3
FFN_CONV_LEFT = 1
N_IN = 2 * D_RNN + Q_LORA + KV_LORA + QK_ROPE + 2 * D_MODEL
LN_EPS = 1e-5
RMS_EPS = 1e-6
DN_ALPHA = (2.0 * DEPTH) ** 0.25
DN_BETA = (8.0 * DEPTH) ** -0.25

kernel_name = "hybrid_rglru_mla_convffn_encoder"


def ln_plain(x):
    xf = x.astype(jnp.float32)
    mu = jnp.mean(xf, axis=-1, keepdims=True)
    var = jnp.mean(jnp.square(xf - mu), axis=-1, keepdims=True)
    return ((xf - mu) * lax.rsqrt(var + LN_EPS)).astype(x.dtype)


def layer_norm(x, g, b):
    xf = x.astype(jnp.float32)
    mu = jnp.mean(xf, axis=-1, keepdims=True)
    var = jnp.mean(jnp.square(xf - mu), axis=-1, keepdims=True)
    y = (xf - mu) * lax.rsqrt(var + LN_EPS) * g.astype(jnp.float32) + b.astype(jnp.float32)
    return y.astype(x.dtype)


def rms_norm(x, g):
    xf = x.astype(jnp.float32)
    y = xf * lax.rsqrt(jnp.mean(jnp.square(xf), axis=-1, keepdims=True) + RMS_EPS)
    return (y * g.astype(jnp.float32)).astype(x.dtype)


def depthwise_conv(x, w, b, left):
    k, c = w.shape
    y = lax.conv_general_dilated(
        x, w[:, None, :].astype(x.dtype), window_strides=(1,),
        padding=[(left, k - 1 - left)], dimension_numbers=("NWC", "WIO", "NWC"),
        feature_group_count=c)
    return y + b.astype(x.dtype)


def rope_tables(s):
    inv = 1.0 / (ROPE_BASE ** (jnp.arange(0, QK_ROPE, 2, dtype=jnp.float32) / QK_ROPE))
    ang = jnp.arange(s, dtype=jnp.float32)[:, None] * inv[None, :]
    return jnp.cos(ang), jnp.sin(ang)


def apply_rope(x, cos, sin):
    xf = x.astype(jnp.float32)
    x1, x2 = jnp.split(xf, 2, axis=-1)
    return jnp.concatenate([x1 * cos - x2 * sin, x1 * sin + x2 * cos], axis=-1).astype(x.dtype)


def _lin_combine(e1, e2):
    a1, b1 = e1
    a2, b2 = e2
    return a1 * a2, a2 * b1 + b2


def rg_lru(x, w_r, b_r, w_i, b_i, lam):
    bsz, s, _ = x.shape
    xb = x.reshape(bsz, s, RNN_BLOCKS, RNN_BLOCK)
    r = jax.nn.sigmoid((jnp.einsum("bsnk,dnkj->dbsnj", xb, w_r).reshape(2, bsz, s, D_RNN)
                        + b_r[:, None, None, :]).astype(jnp.float32))
    i = jax.nn.sigmoid((jnp.einsum("bsnk,dnkj->dbsnj", xb, w_i).reshape(2, bsz, s, D_RNN)
                        + b_i[:, None, None, :]).astype(jnp.float32))
    log_a = -LRU_C * r * jax.nn.softplus(-lam.astype(jnp.float32))[:, None, None, :]
    a = jnp.exp(log_a)
    mult = jnp.sqrt(-jnp.expm1(2.0 * log_a))
    u = mult * i * x.astype(jnp.float32)[None]
    h_fwd = lax.associative_scan(_lin_combine, (a[0], u[0]), axis=1)[1]
    h_bwd = lax.associative_scan(_lin_combine, (a[1], u[1]), axis=1, reverse=True)[1]
    return (h_fwd + h_bwd).astype(x.dtype)


def mla_attention(q_lat, kv_lat, k_rope, q_norm, w_q_b, kv_norm, w_kv_b):
    bsz, s, _ = q_lat.shape
    cos, sin = rope_tables(s)
    q = (rms_norm(q_lat, q_norm) @ w_q_b).reshape(bsz, s, N_HEADS, QK_NOPE + QK_ROPE)
    q_nope = q[..., :QK_NOPE]
    q_pe = apply_rope(q[..., QK_NOPE:], cos[None, :, None, :], sin[None, :, None, :])
    kv = (rms_norm(kv_lat, kv_norm) @ w_kv_b).reshape(bsz, s, N_HEADS, QK_NOPE + V_HEAD)
    k_nope = kv[..., :QK_NOPE]
    v = kv[..., QK_NOPE:]
    k_pe = apply_rope(k_rope, cos[None], sin[None])
    scale = (QK_NOPE + QK_ROPE) ** -0.5
    nblk = s // Q_BLOCK

    def to_blocks(t):
        return jnp.moveaxis(t.reshape((bsz, nblk, Q_BLOCK) + t.shape[2:]), 1, 0)

    def attend(qb):
        qn, qr = qb
        sc = (jnp.einsum("bqhd,bkhd->bhqk", qn, k_nope, preferred_element_type=jnp.float32)
              + jnp.einsum("bqhr,bkr->bhqk", qr, k_pe, preferred_element_type=jnp.float32))
        p = jax.nn.softmax(sc * scale, axis=-1).astype(v.dtype)
        return jnp.einsum("bhqk,bkhd->bqhd", p, v)

    o = lax.map(attend, (to_blocks(q_nope), to_blocks(q_pe)))
    return jnp.moveaxis(o, 0, 1).reshape(bsz, s, N_HEADS * V_HEAD)


def encoder_layer(x, c, p):
    mod = jax.nn.silu(c) @ p["w_ada"] + p["b_ada"]
    sh1, sc1, g1, sh2, sc2, g2 = [m[:, None, :] for m in jnp.split(mod, 6, axis=-1)]

    u = ln_plain(x) * (1.0 + sc1) + sh1
    z = u @ p["w_in"]
    offs = [D_RNN, 2 * D_RNN, 2 * D_RNN + Q_LORA, 2 * D_RNN + Q_LORA + KV_LORA,
            2 * D_RNN + Q_LORA + KV_LORA + QK_ROPE, 2 * D_RNN + Q_LORA + KV_LORA + QK_ROPE + D_MODEL]
    xr, gr, q_lat, kv_lat, k_rope, g_a, g_b = jnp.split(z, offs, axis=-1)
    xr = depthwise_conv(xr, p["rnn_conv_w"], p["rnn_conv_b"], RNN_CONV_LEFT)
    h = rg_lru(xr, p["lru_w_r"], p["lru_b_r"], p["lru_w_i"], p["lru_b_i"], p["lru_lam"])
    y_a = (jax.nn.gelu(gr) * h) @ p["w_rnn_proj"]
    o_b = mla_attention(q_lat, kv_lat, k_rope, p["q_norm"], p["w_q_b"], p["kv_norm"], p["w_kv_b"])
    y_b = o_b @ p["w_mla_proj"]
    mix = (jax.nn.sigmoid(g_a) * y_a + jax.nn.sigmoid(g_b) * y_b) @ p["w_out"]
    x = layer_norm(DN_ALPHA * x + g1 * mix, p["ln1_g"], p["ln1_b"])

    u = ln_plain(x) * (1.0 + sc2) + sh2
    hff = depthwise_conv(u @ p["w_up"], p["ffn_conv_w"], p["ffn_conv_b"], FFN_CONV_LEFT)
    hv, hg = jnp.split(hff, 2, axis=-1)
    y = (jax.nn.silu(hg) * hv) @ p["w_down"]
    return layer_norm(DN_ALPHA * x + g2 * y, p["ln2_g"], p["ln2_b"])


def setup_inputs(seed: int = 0) -> dict:
    key = jax.random.key(seed)
    ks = jax.random.split(key, 32)
    L = DEPTH
    f32 = jnp.float32

    def nrm(k, shape, std):
        return jax.random.normal(k, shape, f32) * std

    u_lam = jax.random.uniform(ks[10], (L, 2, D_RNN), f32, minval=0.9, maxval=0.999)
    s_lam = u_lam ** (1.0 / LRU_C)
    lru_lam = jnp.log(s_lam) - jnp.log1p(-s_lam)
    return {
        "x_prompt": nrm(ks[0], (BATCH, SEQ, D_MODEL), 1.0),
        "x_sample": nrm(ks[1], (DEC_BATCH, DEC_SEQ, D_MODEL), 1.0),
        "c_prompt": nrm(ks[2], (BATCH, D_MODEL), 1.0),
        "c_sample": nrm(ks[3], (DEC_BATCH, D_MODEL), 1.0),
        "w_ada": nrm(ks[4], (L, D_MODEL, 6 * D_MODEL), 0.5 * D_MODEL ** -0.5),
        "b_ada": nrm(ks[5], (L, 6 * D_MODEL), 0.02),
        "w_in": nrm(ks[6], (L, D_MODEL, N_IN), D_MODEL ** -0.5),
        "rnn_conv_w": nrm(ks[7], (L, RNN_CONV, D_RNN), RNN_CONV ** -0.5),
        "rnn_conv_b": nrm(ks[8], (L, D_RNN), 0.02),
        "lru_w_r": nrm(ks[9], (L, 2, RNN_BLOCKS, RNN_BLOCK, RNN_BLOCK), RNN_BLOCK ** -0.5),
        "lru_b_r": nrm(ks[11], (L, 2, D_RNN), 0.02),
        "lru_w_i": nrm(ks[12], (L, 2, RNN_BLOCKS, RNN_BLOCK, RNN_BLOCK), RNN_BLOCK ** -0.5),
        "lru_b_i": nrm(ks[13], (L, 2, D_RNN), 0.02),
        "lru_lam": lru_lam,
        "w_rnn_proj": nrm(ks[14], (L, D_RNN, D_MODEL), D_RNN ** -0.5),
        "q_norm": 1.0 + nrm(ks[15], (L, Q_LORA), 0.02),
        "w_q_b": nrm(ks[16], (L, Q_LORA, N_HEADS * (QK_NOPE + QK_ROPE)), Q_LORA ** -0.5),
        "kv_norm": 1.0 + nrm(ks[17], (L, KV_LORA), 0.02),
        "w_kv_b": nrm(ks[18], (L, KV_LORA, N_HEADS * (QK_NOPE + V_HEAD)), KV_LORA ** -0.5),
        "w_mla_proj": nrm(ks[19], (L, N_HEADS * V_HEAD, D_MODEL), (N_HEADS * V_HEAD) ** -0.5),
        "w_out": nrm(ks[20], (L, D_MODEL, D_MODEL), DN_BETA * D_MODEL ** -0.5),
        "ln1_g": 1.0 + nrm(ks[21], (L, D_MODEL), 0.02),
        "ln1_b": nrm(ks[22], (L, D_MODEL), 0.02),
        "w_up": nrm(ks[23], (L, D_MODEL, 2 * D_FF), D_MODEL ** -0.5),
        "ffn_conv_w": nrm(ks[24], (L, FFN_CONV, 2 * D_FF), FFN_CONV ** -0.5),
        "ffn_conv_b": nrm(ks[25], (L, 2 * D_FF), 0.02),
        "w_down": nrm(ks[26], (L, D_FF, D_MODEL), DN_BETA * D_FF ** -0.5),
        "ln2_g": 1.0 + nrm(ks[27], (L, D_MODEL), 0.02),
        "ln2_b": nrm(ks[28], (L, D_MODEL), 0.02),
    }


def reference(x_prompt, x_sample, c_prompt, c_sample, w_ada, b_ada, w_in, rnn_conv_w, rnn_conv_b,
              lru_w_r, lru_b_r, lru_w_i, lru_b_i, lru_lam, w_rnn_proj, q_norm, w_q_b, kv_norm, w_kv_b,
              w_mla_proj, w_out, ln1_g, ln1_b, w_up, ffn_conv_w, ffn_conv_b, w_down, ln2_g, ln2_b):
    y_prompt = x_prompt
    y_sample = x_sample
    for l in range(DEPTH):
        p = {
            "w_ada": w_ada[l], "b_ada": b_ada[l], "w_in": w_in[l],
            "rnn_conv_w": rnn_conv_w[l], "rnn_conv_b": rnn_conv_b[l],
            "lru_w_r": lru_w_r[l], "lru_b_r": lru_b_r[l], "lru_w_i": lru_w_i[l], "lru_b_i": lru_b_i[l],
            "lru_lam": lru_lam[l], "w_rnn_proj": w_rnn_proj[l],
            "q_norm": q_norm[l], "w_q_b": w_q_b[l], "kv_norm": kv_norm[l], "w_kv_b": w_kv_b[l],
            "w_mla_proj": w_mla_proj[l], "w_out": w_out[l], "ln1_g": ln1_g[l], "ln1_b": ln1_b[l],
            "w_up": w_up[l], "ffn_conv_w": ffn_conv_w[l], "ffn_conv_b": ffn_conv_b[l],
            "w_down": w_down[l], "ln2_g": ln2_g[l], "ln2_b": ln2_b[l],
        }
        y_prompt = encoder_layer(y_prompt, c_prompt, p)
        y_sample = encoder_layer(y_sample, c_sample, p)
    return (y_prompt, y_sample)
```

```python
import functools
import math

import jax
import jax.numpy as jnp
from jax import lax
from jax.experimental import pallas as pl
from jax.experimental.pallas import tpu as pltpu

F32 = jnp.float32
BF16 = jnp.bfloat16

D_MODEL = 1024
D_RNN = 1024
RNN_BLOCKS = 16
RNN_BLOCK = D_RNN // RNN_BLOCKS
RNN_CONV = 4
RNN_CONV_LEFT = 2
LRU_C = 8.0
N_HEADS = 16
QK_NOPE = 64
QK_ROPE = 32
V_HEAD = 64
Q_LORA = 384
KV_LORA = 256
ROPE_BASE = 10000.0
D_FF = 2816
FFN_CONV = 3
LN_EPS = 1e-5
RMS_EPS = 1e-6
DEPTH = 1
DN_ALPHA = (2.0 * DEPTH) ** 0.25

LANES = 128
SUBLANES = 8
BF16_ROWS = 16
MXU_DIM = 256
VMEM_LIMIT = 56 * 1024 * 1024

HEAD_PAD = LANES
HALF_ROPE = QK_ROPE // 2
GATE_GROUP = MXU_DIM
N_GATE_GROUPS = D_RNN // GATE_GROUP
FF_CHUNK = MXU_DIM
N_FF_CHUNKS = D_FF // FF_CHUNK
OFF_QL = 4 * D_MODEL
OFF_KVL = OFF_QL + Q_LORA
OFF_KR = OFF_KVL + KV_LORA
N_IN_PAD = OFF_KR + HEAD_PAD


def _const_spec(shape):
    nd = len(shape)
    return pl.BlockSpec(shape, lambda *_: (0,) * nd, pipeline_mode=pl.Buffered(1))


def _params(n_grid):
    return pltpu.CompilerParams(dimension_semantics=("arbitrary",) * n_grid,
                                vmem_limit_bytes=VMEM_LIMIT)


def _ln_plain(x):
    mu = jnp.mean(x, axis=-1, keepdims=True)
    xc = x - mu
    var = jnp.mean(xc * xc, axis=-1, keepdims=True)
    return xc * lax.rsqrt(var + LN_EPS)


def _rms(x, g):
    return x * lax.rsqrt(jnp.mean(x * x, axis=-1, keepdims=True) + RMS_EPS) * g


def _dot(a, b):
    return jnp.dot(a, b, preferred_element_type=F32)


def _ada_kernel(c_ref, w_ref, b_ref, o_ref):
    c = c_ref[...]
    s = (c * jax.nn.sigmoid(c)).astype(BF16)
    o_ref[...] = _dot(s, w_ref[...].astype(BF16)) + b_ref[...]


def _ada(c_all, w_ada, b_ada, tn=512):
    rows, n = c_all.shape[0], w_ada.shape[1]
    return pl.pallas_call(
        _ada_kernel,
        grid=(n // tn,),
        in_specs=[pl.BlockSpec((rows, D_MODEL), lambda j: (0, 0)),
                  pl.BlockSpec((D_MODEL, tn), lambda j: (0, j)),
                  pl.BlockSpec((1, tn), lambda j: (0, j))],
        out_specs=pl.BlockSpec((rows, tn), lambda j: (0, j)),
        out_shape=jax.ShapeDtypeStruct((rows, n), F32),
        compiler_params=_params(1),
        name="ada",
    )(c_all, w_ada, b_ada)


def _inproj_kernel(x_ref, mod_ref, rope_ref, w1_ref, qn_ref, wq_ref, kvn_ref, wk_ref, wv_ref,
                   xr_ref, gg_ref, sa_ref, sb_ref, q_ref, k_ref, v_ref):
    x = x_ref[...]
    mod = mod_ref[...]
    ub = (_ln_plain(x) * (1.0 + mod[1:2]) + mod[0:1]).astype(BF16)

    def proj(lo, hi):
        return _dot(ub, w1_ref[:, lo:hi])

    xr_ref[...] = proj(0, D_MODEL)
    gg_ref[...] = jax.nn.gelu(proj(D_MODEL, 2 * D_MODEL))
    sa_ref[...] = jax.nn.sigmoid(proj(2 * D_MODEL, 3 * D_MODEL))
    sb_ref[...] = jax.nn.sigmoid(proj(3 * D_MODEL, 4 * D_MODEL))
    qn = _rms(proj(OFF_QL, OFF_KVL), qn_ref[...]).astype(BF16)
    kvn = _rms(proj(OFF_KVL, OFF_KR), kvn_ref[...]).astype(BF16)
    kr = proj(OFF_KR, N_IN_PAD)

    cos_t, sin_lo, sin_hi = rope_ref[0], rope_ref[1], rope_ref[2]

    def rope(t):
        return (t * cos_t + pltpu.roll(t, HEAD_PAD - HALF_ROPE, 1) * sin_lo
                + pltpu.roll(t, HALF_ROPE, 1) * sin_hi)

    kpe = rope(kr)
    qf = _dot(qn, wq_ref[...])
    kf = _dot(kvn, wk_ref[...])
    for h in range(N_HEADS):
        sl = slice(h * HEAD_PAD, (h + 1) * HEAD_PAD)
        q_ref[:, sl] = rope(qf[:, sl]).astype(BF16)
        k_ref[:, sl] = (kf[:, sl] + kpe).astype(BF16)
    v_ref[...] = _dot(kvn, wv_ref[...]).astype(BF16)


def _inproj(x, mod, rope_tab, w1, qn, wq, kvn, wk, wv, tm):
    b, s, _ = x.shape
    tok = lambda w: pl.BlockSpec((None, tm, w), lambda bi, i: (bi, i, 0))
    shp = lambda w, dt: jax.ShapeDtypeStruct((b, s, w), dt)
    return pl.pallas_call(
        _inproj_kernel,
        grid=(b, s // tm),
        in_specs=[tok(D_MODEL),
                  pl.BlockSpec((None, 8, D_MODEL), lambda bi, i: (bi, 0, 0)),
                  pl.BlockSpec((3, tm, HEAD_PAD), lambda bi, i: (0, i, 0)),
                  _const_spec(w1.shape), _const_spec(qn.shape), _const_spec(wq.shape),
                  _const_spec(kvn.shape), _const_spec(wk.shape), _const_spec(wv.shape)],
        out_specs=[tok(D_RNN), tok(D_RNN), tok(D_MODEL), tok(D_MODEL),
                   tok(N_HEADS * HEAD_PAD), tok(N_HEADS * HEAD_PAD), tok(N_HEADS * V_HEAD)],
        out_shape=[shp(D_RNN, F32), shp(D_RNN, F32), shp(D_MODEL, F32), shp(D_MODEL, F32),
                   shp(N_HEADS * HEAD_PAD, BF16), shp(N_HEADS * HEAD_PAD, BF16),
                   shp(N_HEADS * V_HEAD, BF16)],
        compiler_params=_params(2),
        name="inproj",
    )(x, mod, rope_tab, w1, qn, wq, kvn, wk, wv)


def _lru_kernel(xf_ref, xfp_ref, xfn_ref, xb_ref, xbp_ref, xbn_ref,
                cw_ref, cb_ref, wg_ref, br_ref, bi_ref, lam_ref,
                hf_ref, hb_ref, xs_ref, a_ref, u_ref, carry_ref, *, ts):
    c = pl.program_id(1)
    last = pl.num_programs(1) - 1

    @pl.when(c == 0)
    def _():
        carry_ref[...] = jnp.zeros_like(carry_ref)

    lam = lam_ref[...]
    sp = jnp.maximum(-lam, 0.0) + jnp.log1p(jnp.exp(-jnp.abs(lam)))
    cw = cw_ref[...]
    row = lax.broadcasted_iota(jnp.int32, (SUBLANES, D_RNN), 0)
    n_grp = ts // SUBLANES

    def gates(d, cur_ref, prev_ref, next_ref, keep_prev, keep_next):
        xs_ref[0:SUBLANES, :] = prev_ref[...] * keep_prev
        xs_ref[SUBLANES:SUBLANES + ts, :] = cur_ref[...]
        xs_ref[SUBLANES + ts:, :] = next_ref[...] * keep_next
        xc = cb_ref[...] + cw[0:1] * xs_ref[pl.ds(SUBLANES - RNN_CONV_LEFT, ts), :]
        for k in range(1, RNN_CONV):
            xc = xc + cw[k:k + 1] * xs_ref[pl.ds(SUBLANES - RNN_CONV_LEFT + k, ts), :]
        xcb = xc.astype(BF16)
        for g in range(N_GATE_GROUPS):
            sl = slice(g * GATE_GROUP, (g + 1) * GATE_GROUP)
            pre = _dot(xcb[:, sl], wg_ref[d, g])
            r = jax.nn.sigmoid(pre[:, :GATE_GROUP] + br_ref[d:d + 1, sl])
            i = jax.nn.sigmoid(pre[:, GATE_GROUP:] + bi_ref[d:d + 1, sl])
            log_a = (-LRU_C) * r * sp[d:d + 1, sl]
            a = jnp.exp(log_a)
            a_ref[d, :, sl] = a
            u_ref[d, :, sl] = jnp.sqrt(-jnp.tanh(log_a) * (a * a + 1.0)) * i * xc[:, sl]

    def scan(d, out_ref):
        def body(j, carry):
            g = j if d == 0 else n_grp - 1 - j
            off = pl.multiple_of(g * SUBLANES, SUBLANES)
            a8 = a_ref[d, pl.ds(off, SUBLANES), :]
            u8 = u_ref[d, pl.ds(off, SUBLANES), :]
            for sh in (1, 2, 4):
                if d == 0:
                    edge = row < sh
                    shift = sh
                else:
                    edge = row >= SUBLANES - sh
                    shift = SUBLANES - sh
                a_s = jnp.where(edge, 1.0, pltpu.roll(a8, shift, 0))
                u_s = jnp.where(edge, 0.0, pltpu.roll(u8, shift, 0))
                u8 = a8 * u_s + u8
                a8 = a8 * a_s
            h8 = u8 + a8 * carry
            out_ref[pl.ds(off, SUBLANES), :] = h8
            return h8[SUBLANES - 1:SUBLANES, :] if d == 0 else h8[0:1, :]

        carry_ref[d:d + 1, :] = lax.fori_loop(0, n_grp, body, carry_ref[d:d + 1, :])

    not_first = (c > 0).astype(F32)
    not_last = (c < last).astype(F32)
    gates(0, xf_ref, xfp_ref, xfn_ref, not_first, not_last)
    gates(1, xb_ref, xbp_ref, xbn_ref, not_last, not_first)
    scan(0, hf_ref)
    scan(1, hb_ref)


def _lru(xr, cw, cb, wg, br, bi, lam, ts):
    b, s, _ = xr.shape
    nc = s // ts
    hpb = ts // SUBLANES
    nhb = s // SUBLANES
    cur = lambda f: pl.BlockSpec((None, ts, D_RNN), lambda bi_, c: (bi_, f(c), 0))
    halo = lambda f: pl.BlockSpec((None, SUBLANES, D_RNN), lambda bi_, c: (bi_, f(c), 0))
    fwd = lambda c: c
    bwd = lambda c: nc - 1 - c
    prev = lambda f: (lambda c: jnp.maximum(f(c) * hpb - 1, 0))
    nxt = lambda f: (lambda c: jnp.minimum((f(c) + 1) * hpb, nhb - 1))
    return pl.pallas_call(
        functools.partial(_lru_kernel, ts=ts),
        grid=(b, nc),
        in_specs=[cur(fwd), halo(prev(fwd)), halo(nxt(fwd)),
                  cur(bwd), halo(prev(bwd)), halo(nxt(bwd)),
                  _const_spec(cw.shape), _const_spec(cb.shape), _const_spec(wg.shape),
                  _const_spec(br.shape), _const_spec(bi.shape), _const_spec(lam.shape)],
        out_specs=[cur(fwd), cur(bwd)],
        out_shape=[jax.ShapeDtypeStruct((b, s, D_RNN), F32)] * 2,
        scratch_shapes=[pltpu.VMEM((ts + 2 * SUBLANES, D_RNN), F32),
                        pltpu.VMEM((2, ts, D_RNN), F32),
                        pltpu.VMEM((2, ts, D_RNN), F32),
                        pltpu.VMEM((SUBLANES, D_RNN), F32)],
        compiler_params=_params(2),
        name="lru",
    )(xr, xr, xr, xr, xr, xr, cw, cb, wg, br, bi, lam)


def _attn_kernel(q_ref, k_ref, v_ref, o_ref, m_ref, l_ref, acc_ref, *, tk):
    s_len = k_ref.shape[0]
    n_kv = s_len // tk
    tq = q_ref.shape[0]
    lane = lax.broadcasted_iota(jnp.int32, (tq, 2 * V_HEAD), 1)
    outs = []
    for hh in range(2):
        hs = slice(hh * HEAD_PAD, (hh + 1) * HEAD_PAD)
        q = q_ref[:, hs]
        m_ref[...] = jnp.full_like(m_ref, -jnp.inf)
        l_ref[...] = jnp.zeros_like(l_ref)
        acc_ref[...] = jnp.zeros_like(acc_ref)

        def body(j, _):
            off = pl.multiple_of(j * tk, tk)
            k = k_ref[pl.ds(off, tk), hs]
            s = lax.dot_general(q, k, (((1,), (1,)), ((), ())), preferred_element_type=F32)
            m_old = m_ref[...]
            m_new = jnp.maximum(m_old, jnp.max(s, axis=1, keepdims=True))
            alpha = jnp.exp2(m_old - m_new)
            p = jnp.exp2(s - m_new)
            l_ref[...] = alpha * l_ref[...] + jnp.sum(p, axis=1, keepdims=True)
            acc_ref[...] = alpha * acc_ref[...] + _dot(p.astype(BF16), v_ref[pl.ds(off, tk), :])
            m_ref[...] = m_new
            return 0

        lax.fori_loop(0, n_kv, body, 0)
        outs.append(acc_ref[...] / l_ref[...])
    o_ref[...] = jnp.where(lane < V_HEAD, outs[0], outs[1]).astype(o_ref.dtype)


def _attn(q, k, v, tq, tk):
    b, s, _ = q.shape
    return pl.pallas_call(
        functools.partial(_attn_kernel, tk=tk),
        grid=(b, N_HEADS // 2, s // tq),
        in_specs=[pl.BlockSpec((None, tq, 2 * HEAD_PAD), lambda bi, p, i: (bi, i, p)),
                  pl.BlockSpec((None, s, 2 * HEAD_PAD), lambda bi, p, i: (bi, 0, p)),
                  pl.BlockSpec((None, s, 2 * V_HEAD), lambda bi, p, i: (bi, 0, p))],
        out_specs=pl.BlockSpec((None, tq, 2 * V_HEAD), lambda bi, p, i: (bi, i, p)),
        out_shape=jax.ShapeDtypeStruct((b, s, N_HEADS * V_HEAD), BF16),
        scratch_shapes=[pltpu.VMEM((tq, 1), F32), pltpu.VMEM((tq, 1), F32),
                        pltpu.VMEM((tq, 2 * V_HEAD), F32)],
        compiler_params=_params(3),
        name="attn",
    )(q, k, v)


def _merge_kernel(x_ref, hf_ref, hb_ref, gg_ref, sa_ref, sb_ref, o_ref, mod_ref,
                  wr_ref, wm_ref, wo_ref, g_ref, b_ref, x1_ref, u2_ref):
    mod = mod_ref[...]
    ya = _dot((gg_ref[...] * (hf_ref[...] + hb_ref[...])).astype(BF16), wr_ref[...])
    yb = _dot(o_ref[...], wm_ref[...])
    mix = _dot((sa_ref[...] * ya + sb_ref[...] * yb).astype(BF16), wo_ref[...])
    x1 = _ln_plain(DN_ALPHA * x_ref[...] + mod[2:3] * mix) * g_ref[...] + b_ref[...]
    x1_ref[...] = x1
    u2_ref[...] = (_ln_plain(x1) * (1.0 + mod[4:5]) + mod[3:4]).astype(BF16)


def _merge(x, hf, hb, gg, sa, sb, o, mod, wr, wm, wo, g1, b1, tm):
    b, s, _ = x.shape
    tok = pl.BlockSpec((None, tm, D_MODEL), lambda bi, i: (bi, i, 0))
    return pl.pallas_call(
        _merge_kernel,
        grid=(b, s // tm),
        in_specs=[tok] * 7 + [pl.BlockSpec((None, 8, D_MODEL), lambda bi, i: (bi, 0, 0)),
                              _const_spec(wr.shape), _const_spec(wm.shape), _const_spec(wo.shape),
                              _const_spec(g1.shape), _const_spec(b1.shape)],
        out_specs=[tok, tok],
        out_shape=[jax.ShapeDtypeStruct((b, s, D_MODEL), F32),
                   jax.ShapeDtypeStruct((b, s, D_MODEL), BF16)],
        compiler_params=_params(2),
        name="merge",
    )(x, hf, hb, gg, sa, sb, o, mod, wr, wm, wo, g1, b1)


def _ffn_kernel(u_ref, up_ref, un_ref, x1_ref, mod_ref, wv_ref, wg_ref, cw_ref, cb_ref, wd_ref,
                g_ref, b_ref, y_ref, ue_ref, hv_ref, hg_ref, *, tm):
    i = pl.program_id(1)
    last = pl.num_programs(1) - 1
    halo = BF16_ROWS
    ue_ref[0:halo, :] = up_ref[...]
    ue_ref[halo:halo + tm, :] = u_ref[...]
    ue_ref[halo + tm:, :] = un_ref[...]
    ue = ue_ref[...]
    acc = jnp.zeros((tm, D_MODEL), F32)
    for j in range(N_FF_CHUNKS):
        sl = slice(j * FF_CHUNK, (j + 1) * FF_CHUNK)
        slg = slice(D_FF + j * FF_CHUNK, D_FF + (j + 1) * FF_CHUNK)
        hv_ref[...] = _dot(ue, wv_ref[:, sl])
        hg_ref[...] = _dot(ue, wg_ref[:, sl])

        @pl.when(i == 0)
        def _():
            hv_ref[0:halo, :] = jnp.zeros((halo, FF_CHUNK), F32)
            hg_ref[0:halo, :] = jnp.zeros((halo, FF_CHUNK), F32)

        @pl.when(i == last)
        def _():
            hv_ref[halo + tm:, :] = jnp.zeros((halo, FF_CHUNK), F32)
            hg_ref[halo + tm:, :] = jnp.zeros((halo, FF_CHUNK), F32)

        def conv(h_ref, w, bias):
            return (bias + w[0:1] * h_ref[pl.ds(halo - 1, tm), :]
                    + w[1:2] * h_ref[pl.ds(halo, tm), :]
                    + w[2:3] * h_ref[pl.ds(halo + 1, tm), :])

        cv = conv(hv_ref, cw_ref[:, sl], cb_ref[:, sl])
        cg = conv(hg_ref, cw_ref[:, slg], cb_ref[:, slg])
        act = (cg * jax.nn.sigmoid(cg) * cv).astype(BF16)
        acc = acc + _dot(act, wd_ref[sl, :])
    mod = mod_ref[...]
    y_ref[...] = _ln_plain(DN_ALPHA * x1_ref[...] + mod[5:6] * acc) * g_ref[...] + b_ref[...]


def _ffn(u2, x1, mod, wv, wg, cw, cb, wd, g2, b2, tm):
    b, s, _ = x1.shape
    hpb = tm // BF16_ROWS
    nhb = s // BF16_ROWS
    tok = pl.BlockSpec((None, tm, D_MODEL), lambda bi, i: (bi, i, 0))
    return pl.pallas_call(
        functools.partial(_ffn_kernel, tm=tm),
        grid=(b, s // tm),
        in_specs=[tok,
                  pl.BlockSpec((None, BF16_ROWS, D_MODEL),
                               lambda bi, i: (bi, jnp.maximum(i * hpb - 1, 0), 0)),
                  pl.BlockSpec((None, BF16_ROWS, D_MODEL),
                               lambda bi, i: (bi, jnp.minimum((i + 1) * hpb, nhb - 1), 0)),
                  tok,
                  pl.BlockSpec((None, 8, D_MODEL), lambda bi, i: (bi, 0, 0)),
                  _const_spec(wv.shape), _const_spec(wg.shape), _const_spec(cw.shape),
                  _const_spec(cb.shape), _const_spec(wd.shape), _const_spec(g2.shape),
                  _const_spec(b2.shape)],
        out_specs=tok,
        out_shape=jax.ShapeDtypeStruct((b, s, D_MODEL), F32),
        scratch_shapes=[pltpu.VMEM((tm + 2 * BF16_ROWS, D_MODEL), BF16),
                        pltpu.VMEM((tm + 2 * BF16_ROWS, FF_CHUNK), F32),
                        pltpu.VMEM((tm + 2 * BF16_ROWS, FF_CHUNK), F32)],
        compiler_params=_params(2),
        name="ffn",
    )(u2, u2, u2, x1, mod, wv, wg, cw, cb, wd, g2, b2)


def _rope_table(s):
    inv = 1.0 / (ROPE_BASE ** (jnp.arange(0, QK_ROPE, 2, dtype=F32) / QK_ROPE))
    ang = jnp.arange(s, dtype=F32)[:, None] * inv[None, :]
    cos, sin = jnp.cos(ang), jnp.sin(ang)
    one = jnp.ones((s, QK_NOPE), F32)
    zero = jnp.zeros((s, HALF_ROPE), F32)
    tail = jnp.zeros((s, HEAD_PAD - QK_NOPE - QK_ROPE), F32)
    zn = jnp.zeros((s, QK_NOPE), F32)
    cos_t = jnp.concatenate([one, cos, cos, tail], axis=1)
    sin_lo = jnp.concatenate([zn, -sin, zero, tail], axis=1)
    sin_hi = jnp.concatenate([zn, zero, sin, tail], axis=1)
    return jnp.stack([cos_t, sin_lo, sin_hi])


def _prep_weights(w_in, lru_w_r, lru_w_i, w_q_b, w_kv_b, w_up):
    o1, o2 = D_RNN, 2 * D_RNN
    o3, o4, o5 = o2 + Q_LORA, o2 + Q_LORA + KV_LORA, o2 + Q_LORA + KV_LORA + QK_ROPE
    o6 = o5 + D_MODEL
    kr = jnp.pad(w_in[:, o4:o5], ((0, 0), (QK_NOPE, HEAD_PAD - QK_NOPE - QK_ROPE)))
    w1 = jnp.concatenate([w_in[:, :o2], w_in[:, o5:o6], w_in[:, o6:], w_in[:, o2:o4], kr],
                         axis=1).astype(BF16)
    eye = jnp.eye(RNN_BLOCKS // N_GATE_GROUPS, dtype=F32)

    def blockdiag(w):
        wgp = w.reshape(2, N_GATE_GROUPS, RNN_BLOCKS // N_GATE_GROUPS, RNN_BLOCK, RNN_BLOCK)
        return jnp.einsum("dgnkj,nm->dgnkmj", wgp, eye).reshape(2, N_GATE_GROUPS, GATE_GROUP, GATE_GROUP)

    wgate = jnp.concatenate([blockdiag(lru_w_r), blockdiag(lru_w_i)], axis=-1).astype(BF16)
    qscale = (QK_NOPE + QK_ROPE) ** -0.5 * math.log2(math.e)
    wq = jnp.pad((w_q_b * qscale).reshape(Q_LORA, N_HEADS, QK_NOPE + QK_ROPE),
                 ((0, 0), (0, 0), (0, HEAD_PAD - QK_NOPE - QK_ROPE)))
    wq = wq.reshape(Q_LORA, N_HEADS * HEAD_PAD).astype(BF16)
    wkv = w_kv_b.reshape(KV_LORA, N_HEADS, QK_NOPE + V_HEAD)
    wk = jnp.pad(wkv[:, :, :QK_NOPE], ((0, 0), (0, 0), (0, HEAD_PAD - QK_NOPE)))
    wk = wk.reshape(KV_LORA, N_HEADS * HEAD_PAD).astype(BF16)
    wv = wkv[:, :, QK_NOPE:].reshape(KV_LORA, N_HEADS * V_HEAD).astype(BF16)
    return w1, wgate, wq, wk, wv, w_up[:, :D_FF].astype(BF16), w_up[:, D_FF:].astype(BF16)


def _tile(s, pref):
    return min(s, pref)


def _layer(x, mod, p, *, tm_in=256, ts=256, tq=256, tk=512, tm_merge=256, tm_ffn=256):
    s = x.shape[1]
    rope_tab = _rope_table(s)
    xr, gg, sa, sb, q, k, v = _inproj(x, mod, rope_tab, p["w1"], p["q_norm"], p["wq"], p["kv_norm"],
                                      p["wk"], p["wv"], _tile(s, tm_in))
    hf, hb = _lru(xr, p["rnn_conv_w"], p["rnn_conv_b"], p["wgate"], p["lru_b_r"], p["lru_b_i"],
                  p["lru_lam"], _tile(s, ts))
    o = _attn(q, k, v, _tile(s, tq), _tile(s, tk))
    x1, u2 = _merge(x, hf, hb, gg, sa, sb, o, mod, p["w_rnn_proj"], p["w_mla_proj"], p["w_out"],
                    p["ln1_g"], p["ln1_b"], _tile(s, tm_merge))
    return _ffn(u2, x1, mod, p["w_up_v"], p["w_up_g"], p["ffn_conv_w"], p["ffn_conv_b"], p["w_down"],
                p["ln2_g"], p["ln2_b"], _tile(s, tm_ffn))


def kernel(x_prompt, x_sample, c_prompt, c_sample, w_ada, b_ada, w_in, rnn_conv_w, rnn_conv_b, lru_w_r, lru_b_r, lru_w_i, lru_b_i, lru_lam, w_rnn_proj, q_norm, w_q_b, kv_norm, w_kv_b, w_mla_proj, w_out, ln1_g, ln1_b, w_up, ffn_conv_w, ffn_conv_b, w_down, ln2_g, ln2_b):
    assert w_ada.shape[0] == DEPTH
    l = 0
    nb_p, nb_s = c_prompt.shape[0], c_sample.shape[0]
    rows = -(-(nb_p + nb_s) // SUBLANES) * SUBLANES
    c_all = jnp.concatenate([c_prompt, c_sample, jnp.zeros((rows - nb_p - nb_s, D_MODEL), F32)])
    mod = _ada(c_all, w_ada[l], b_ada[l][None, :]).reshape(rows, 6, D_MODEL)
    mod = jnp.pad(mod, ((0, 0), (0, 2), (0, 0)))
    w1, wgate, wq, wk, wv, w_up_v, w_up_g = _prep_weights(w_in[l], lru_w_r[l], lru_w_i[l], w_q_b[l],
                                                         w_kv_b[l], w_up[l])
    row = lambda a: a[l][None, :]
    p = {
        "w1": w1, "wgate": wgate, "wq": wq, "wk": wk, "wv": wv, "w_up_v": w_up_v, "w_up_g": w_up_g,
        "q_norm": row(q_norm), "kv_norm": row(kv_norm),
        "rnn_conv_w": rnn_conv_w[l], "rnn_conv_b": row(rnn_conv_b),
        "lru_b_r": lru_b_r[l], "lru_b_i": lru_b_i[l], "lru_lam": lru_lam[l],
        "w_rnn_proj": w_rnn_proj[l].astype(BF16), "w_mla_proj": w_mla_proj[l].astype(BF16),
        "w_out": w_out[l].astype(BF16), "ln1_g": row(ln1_g), "ln1_b": row(ln1_b),
        "ffn_conv_w": ffn_conv_w[l], "ffn_conv_b": row(ffn_conv_b),
        "w_down": w_down[l].astype(BF16), "ln2_g": row(ln2_g), "ln2_b": row(ln2_b),
    }
    y_prompt = _layer(x_prompt, mod[:nb_p], p)
    y_sample = _layer(x_sample, mod[nb_p:nb_p + nb_s], p)
    return (y_prompt, y_sample)
```

```python
import functools
import math

import jax
import jax.numpy as jnp
from jax import lax
from jax.experimental import pallas as pl
from jax.experimental.pallas import tpu as pltpu

F32 = jnp.float32
BF16 = jnp.bfloat16

D_MODEL = 1024
D_RNN = 1024
RNN_BLOCKS = 16
RNN_BLOCK = D_RNN // RNN_BLOCKS
RNN_CONV = 4
RNN_CONV_LEFT = 2
LRU_C = 8.0
N_HEADS = 16
QK_NOPE = 64
QK_ROPE = 32
V_HEAD = 64
Q_LORA = 384
KV_LORA = 256
ROPE_BASE = 10000.0
D_FF = 2816
FFN_CONV = 3
LN_EPS = 1e-5
RMS_EPS = 1e-6
DEPTH = 1
DN_ALPHA = (2.0 * DEPTH) ** 0.25

LANES = 128
SUBLANES = 8
BF16_ROWS = 16
MXU_DIM = 256
VMEM_LIMIT = 56 * 1024 * 1024

HEAD_PAD = LANES
HALF_ROPE = QK_ROPE // 2
GATE_GROUP = MXU_DIM
N_GATE_GROUPS = D_RNN // GATE_GROUP
FF_CHUNK = MXU_DIM
N_FF_CHUNKS = D_FF // FF_CHUNK
OFF_QL = 4 * D_MODEL
OFF_KVL = OFF_QL + Q_LORA
OFF_KR = OFF_KVL + KV_LORA
N_IN_PAD = OFF_KR + HEAD_PAD


def _const_spec(shape):
    nd = len(shape)
    return pl.BlockSpec(shape, lambda *_: (0,) * nd, pipeline_mode=pl.Buffered(1))


def _params(n_grid):
    return pltpu.CompilerParams(dimension_semantics=("arbitrary",) * n_grid,
                                vmem_limit_bytes=VMEM_LIMIT)


def _ln_plain(x):
    mu = jnp.mean(x, axis=-1, keepdims=True)
    xc = x - mu
    var = jnp.mean(xc * xc, axis=-1, keepdims=True)
    return xc * lax.rsqrt(var + LN_EPS)


def _rms(x, g):
    return x * lax.rsqrt(jnp.mean(x * x, axis=-1, keepdims=True) + RMS_EPS) * g


def _dot(a, b):
    return jnp.dot(a, b, preferred_element_type=F32)


def _ada_kernel(c_ref, w_ref, b_ref, o_ref):
    c = c_ref[...]
    s = (c * jax.nn.sigmoid(c)).astype(BF16)
    o_ref[...] = _dot(s, w_ref[...].astype(BF16)) + b_ref[...]


def _ada(c_all, w_ada, b_ada, tn=512):
    rows, n = c_all.shape[0], w_ada.shape[1]
    return pl.pallas_call(
        _ada_kernel,
        grid=(n // tn,),
        in_specs=[pl.BlockSpec((rows, D_MODEL), lambda j: (0, 0)),
                  pl.BlockSpec((D_MODEL, tn), lambda j: (0, j)),
                  pl.BlockSpec((1, tn), lambda j: (0, j))],
        out_specs=pl.BlockSpec((rows, tn), lambda j: (0, j)),
        out_shape=jax.ShapeDtypeStruct((rows, n), F32),
        compiler_params=_params(1),
        name="ada",
    )(c_all, w_ada, b_ada)


def _inproj_kernel(x_ref, mod_ref, rope_ref, w1_ref, qn_ref, wq_ref, kvn_ref, wk_ref, wvt_ref,
                   xr_ref, gg_ref, sa_ref, sb_ref, q_ref, k_ref, vt_ref):
    x = x_ref[...]
    mod = mod_ref[...]
    ub = (_ln_plain(x) * (1.0 + mod[1:2]) + mod[0:1]).astype(BF16)

    def proj(lo, hi):
        return _dot(ub, w1_ref[:, lo:hi])

    xr_ref[...] = proj(0, D_MODEL)
    gg_ref[...] = jax.nn.gelu(proj(D_MODEL, 2 * D_MODEL))
    sa_ref[...] = jax.nn.sigmoid(proj(2 * D_MODEL, 3 * D_MODEL))
    sb_ref[...] = jax.nn.sigmoid(proj(3 * D_MODEL, 4 * D_MODEL))
    qn = _rms(proj(OFF_QL, OFF_KVL), qn_ref[...]).astype(BF16)
    kvn = _rms(proj(OFF_KVL, OFF_KR), kvn_ref[...]).astype(BF16)
    kr = proj(OFF_KR, N_IN_PAD)

    cos_t, sin_lo, sin_hi = rope_ref[0], rope_ref[1], rope_ref[2]

    def rope(t):
        return (t * cos_t + pltpu.roll(t, HEAD_PAD - HALF_ROPE, 1) * sin_lo
                + pltpu.roll(t, HALF_ROPE, 1) * sin_hi)

    kpe = rope(kr)
    qf = _dot(qn, wq_ref[...])
    kf = _dot(kvn, wk_ref[...])
    for h in range(N_HEADS):
        sl = slice(h * HEAD_PAD, (h + 1) * HEAD_PAD)
        q_ref[:, sl] = rope(qf[:, sl]).astype(BF16)
        k_ref[:, sl] = (kf[:, sl] + kpe).astype(BF16)
    vt_ref[...] = lax.dot_general(wvt_ref[...], kvn, (((1,), (1,)), ((), ())),
                                  preferred_element_type=F32).astype(BF16)


def _inproj(x, mod, rope_tab, w1, qn, wq, kvn, wk, wv, tm):
    b, s, _ = x.shape
    tok = lambda w: pl.BlockSpec((None, tm, w), lambda bi, i: (bi, i, 0))
    shp = lambda w, dt: jax.ShapeDtypeStruct((b, s, w), dt)
    return pl.pallas_call(
        _inproj_kernel,
        grid=(b, s // tm),
        in_specs=[tok(D_MODEL),
                  pl.BlockSpec((None, 8, D_MODEL), lambda bi, i: (bi, 0, 0)),
                  pl.BlockSpec((3, tm, HEAD_PAD), lambda bi, i: (0, i, 0)),
                  _const_spec(w1.shape), _const_spec(qn.shape), _const_spec(wq.shape),
                  _const_spec(kvn.shape), _const_spec(wk.shape), _const_spec(wv.shape)],
        out_specs=[tok(D_RNN), tok(D_RNN), tok(D_MODEL), tok(D_MODEL),
                   tok(N_HEADS * HEAD_PAD), tok(N_HEADS * HEAD_PAD),
                   pl.BlockSpec((None, N_HEADS * V_HEAD, tm), lambda bi, i: (bi, 0, i))],
        out_shape=[shp(D_RNN, F32), shp(D_RNN, F32), shp(D_MODEL, F32), shp(D_MODEL, F32),
                   shp(N_HEADS * HEAD_PAD, BF16), shp(N_HEADS * HEAD_PAD, BF16),
                   jax.ShapeDtypeStruct((b, N_HEADS * V_HEAD, s), BF16)],
        compiler_params=_params(2),
        name="inproj",
    )(x, mod, rope_tab, w1, qn, wq, kvn, wk, wv)


def _lru_kernel(xf_ref, xfp_ref, xfn_ref, xb_ref, xbp_ref, xbn_ref,
                cw_ref, cb_ref, wg_ref, br_ref, bi_ref, lam_ref,
                hf_ref, hb_ref, xs_ref, a_ref, u_ref, carry_ref, *, ts):
    c = pl.program_id(1)
    last = pl.num_programs(1) - 1

    @pl.when(c == 0)
    def _():
        carry_ref[...] = jnp.zeros_like(carry_ref)

    lam = lam_ref[...]
    sp = jnp.maximum(-lam, 0.0) + jnp.log1p(jnp.exp(-jnp.abs(lam)))
    cw = cw_ref[...]
    row = lax.broadcasted_iota(jnp.int32, (SUBLANES, D_RNN), 0)
    n_grp = ts // SUBLANES

    def gates(d, cur_ref, prev_ref, next_ref, keep_prev, keep_next):
        xs_ref[0:SUBLANES, :] = prev_ref[...] * keep_prev
        xs_ref[SUBLANES:SUBLANES + ts, :] = cur_ref[...]
        xs_ref[SUBLANES + ts:, :] = next_ref[...] * keep_next
        xc = cb_ref[...] + cw[0:1] * xs_ref[pl.ds(SUBLANES - RNN_CONV_LEFT, ts), :]
        for k in range(1, RNN_CONV):
            xc = xc + cw[k:k + 1] * xs_ref[pl.ds(SUBLANES - RNN_CONV_LEFT + k, ts), :]
        xcb = xc.astype(BF16)
        for g in range(N_GATE_GROUPS):
            sl = slice(g * GATE_GROUP, (g + 1) * GATE_GROUP)
            pre = _dot(xcb[:, sl], wg_ref[d, g])
            r = jax.nn.sigmoid(pre[:, :GATE_GROUP] + br_ref[d:d + 1, sl])
            i = jax.nn.sigmoid(pre[:, GATE_GROUP:] + bi_ref[d:d + 1, sl])
            log_a = (-LRU_C) * r * sp[d:d + 1, sl]
            a = jnp.exp(log_a)
            a_ref[d, :, sl] = a
            u_ref[d, :, sl] = jnp.sqrt(-jnp.tanh(log_a) * (a * a + 1.0)) * i * xc[:, sl]

    def scan(d, out_ref):
        def body(j, carry):
            g = j if d == 0 else n_grp - 1 - j
            off = pl.multiple_of(g * SUBLANES, SUBLANES)
            a8 = a_ref[d, pl.ds(off, SUBLANES), :]
            u8 = u_ref[d, pl.ds(off, SUBLANES), :]
            for sh in (1, 2, 4):
                if d == 0:
                    edge = row < sh
                    shift = sh
                else:
                    edge = row >= SUBLANES - sh
                    shift = SUBLANES - sh
                a_s = jnp.where(edge, 1.0, pltpu.roll(a8, shift, 0))
                u_s = jnp.where(edge, 0.0, pltpu.roll(u8, shift, 0))
                u8 = a8 * u_s + u8
                a8 = a8 * a_s
            h8 = u8 + a8 * carry
            out_ref[pl.ds(off, SUBLANES), :] = h8
            return h8[SUBLANES - 1:SUBLANES, :] if d == 0 else h8[0:1, :]

        carry_ref[d:d + 1, :] = lax.fori_loop(0, n_grp, body, carry_ref[d:d + 1, :])

    not_first = (c > 0).astype(F32)
    not_last = (c < last).astype(F32)
    gates(0, xf_ref, xfp_ref, xfn_ref, not_first, not_last)
    gates(1, xb_ref, xbp_ref, xbn_ref, not_last, not_first)
    scan(0, hf_ref)
    scan(1, hb_ref)


def _lru(xr, cw, cb, wg, br, bi, lam, ts):
    b, s, _ = xr.shape
    nc = s // ts
    hpb = ts // SUBLANES
    nhb = s // SUBLANES
    cur = lambda f: pl.BlockSpec((None, ts, D_RNN), lambda bi_, c: (bi_, f(c), 0))
    halo = lambda f: pl.BlockSpec((None, SUBLANES, D_RNN), lambda bi_, c: (bi_, f(c), 0))
    fwd = lambda c: c
    bwd = lambda c: nc - 1 - c
    prev = lambda f: (lambda c: jnp.maximum(f(c) * hpb - 1, 0))
    nxt = lambda f: (lambda c: jnp.minimum((f(c) + 1) * hpb, nhb - 1))
    return pl.pallas_call(
        functools.partial(_lru_kernel, ts=ts),
        grid=(b, nc),
        in_specs=[cur(fwd), halo(prev(fwd)), halo(nxt(fwd)),
                  cur(bwd), halo(prev(bwd)), halo(nxt(bwd)),
                  _const_spec(cw.shape), _const_spec(cb.shape), _const_spec(wg.shape),
                  _const_spec(br.shape), _const_spec(bi.shape), _const_spec(lam.shape)],
        out_specs=[cur(fwd), cur(bwd)],
        out_shape=[jax.ShapeDtypeStruct((b, s, D_RNN), F32)] * 2,
        scratch_shapes=[pltpu.VMEM((ts + 2 * SUBLANES, D_RNN), F32),
                        pltpu.VMEM((2, ts, D_RNN), F32),
                        pltpu.VMEM((2, ts, D_RNN), F32),
                        pltpu.VMEM((SUBLANES, D_RNN), F32)],
        compiler_params=_params(2),
        name="lru",
    )(xr, xr, xr, xr, xr, xr, cw, cb, wg, br, bi, lam)


def _attn_kernel(q_ref, k_ref, vt_ref, o_ref, st_ref, *, tk):
    n_kv = k_ref.shape[0] // tk
    assert n_kv % 2 == 0
    unroll = 4 if n_kv % 4 == 0 else 2
    tq = q_ref.shape[0]
    heads = [slice(hh * HEAD_PAD, (hh + 1) * HEAD_PAD) for hh in range(2)]
    qs = [q_ref[:, hs] for hs in heads]

    def scores(j, hh, slot):
        off = pl.multiple_of(j * tk, tk)
        st = lax.dot_general(k_ref[pl.ds(off, tk), heads[hh]], qs[hh], (((1,), (1,)), ((), ())),
                             preferred_element_type=F32)
        st_ref[slot, hh] = st
        return jnp.max(st, axis=0, keepdims=True)

    ones = jnp.ones((BF16_ROWS, MXU_DIM), BF16)

    def softmax_pv(j, hh, slot, m_old, acc, cmax):
        off = pl.multiple_of(j * tk, tk)
        m_new = jnp.maximum(m_old, cmax)
        acc = jnp.exp2(m_old - m_new) * acc
        for t in range(tk // MXU_DIM):
            rows = slice(t * MXU_DIM, (t + 1) * MXU_DIM)
            pt = jnp.exp2(st_ref[slot, hh, rows, :] - m_new).astype(BF16)
            vt = vt_ref[hh * V_HEAD:(hh + 1) * V_HEAD, pl.ds(off + t * MXU_DIM, MXU_DIM)]
            acc = acc + _dot(jnp.concatenate([vt, ones], axis=0), pt)
        return m_new, acc

    def body(jj, carry):
        for step in range(unroll):
            j = unroll * jj + step
            slot = step % 2
            j_next = jnp.minimum(j + 1, n_kv - 1)
            new = []
            for hh in range(2):
                m_old, acc, cmax = carry[hh]
                cmax_next = scores(j_next, hh, 1 - slot)
                new.append(softmax_pv(j, hh, slot, m_old, acc, cmax) + (cmax_next,))
            carry = tuple(new)
        return carry

    init = tuple((jnp.full((1, tq), -jnp.inf, F32), jnp.zeros((V_HEAD + BF16_ROWS, tq), F32),
                  scores(0, hh, 0)) for hh in range(2))
    res = lax.fori_loop(0, n_kv // unroll, body, init)
    ot = jnp.concatenate([acc[:V_HEAD] / acc[V_HEAD:V_HEAD + 1] for _, acc, _ in res], axis=0)
    o_ref[...] = ot.T.astype(o_ref.dtype)


def _attn(q, k, vt, tq, tk):
    b, s, _ = q.shape
    return pl.pallas_call(
        functools.partial(_attn_kernel, tk=tk),
        scratch_shapes=[pltpu.VMEM((2, 2, tk, tq), F32)],
        grid=(b, N_HEADS // 2, s // tq),
        in_specs=[pl.BlockSpec((None, tq, 2 * HEAD_PAD), lambda bi, p, i: (bi, i, p)),
                  pl.BlockSpec((None, s, 2 * HEAD_PAD), lambda bi, p, i: (bi, 0, p)),
                  pl.BlockSpec((None, 2 * V_HEAD, s), lambda bi, p, i: (bi, p, 0))],
        out_specs=pl.BlockSpec((None, tq, 2 * V_HEAD), lambda bi, p, i: (bi, i, p)),
        out_shape=jax.ShapeDtypeStruct((b, s, N_HEADS * V_HEAD), BF16),
        compiler_params=_params(3),
        name="attn",
    )(q, k, vt)


def _merge_kernel(x_ref, hf_ref, hb_ref, gg_ref, sa_ref, sb_ref, o_ref, mod_ref,
                  wr_ref, wm_ref, wo_ref, g_ref, b_ref, x1_ref, u2_ref):
    mod = mod_ref[...]
    ya = _dot((gg_ref[...] * (hf_ref[...] + hb_ref[...])).astype(BF16), wr_ref[...])
    yb = _dot(o_ref[...], wm_ref[...])
    mix = _dot((sa_ref[...] * ya + sb_ref[...] * yb).astype(BF16), wo_ref[...])
    x1 = _ln_plain(DN_ALPHA * x_ref[...] + mod[2:3] * mix) * g_ref[...] + b_ref[...]
    x1_ref[...] = x1
    u2_ref[...] = (_ln_plain(x1) * (1.0 + mod[4:5]) + mod[3:4]).astype(BF16)


def _merge(x, hf, hb, gg, sa, sb, o, mod, wr, wm, wo, g1, b1, tm):
    b, s, _ = x.shape
    tok = pl.BlockSpec((None, tm, D_MODEL), lambda bi, i: (bi, i, 0))
    return pl.pallas_call(
        _merge_kernel,
        grid=(b, s // tm),
        in_specs=[tok] * 7 + [pl.BlockSpec((None, 8, D_MODEL), lambda bi, i: (bi, 0, 0)),
                              _const_spec(wr.shape), _const_spec(wm.shape), _const_spec(wo.shape),
                              _const_spec(g1.shape), _const_spec(b1.shape)],
        out_specs=[tok, tok],
        out_shape=[jax.ShapeDtypeStruct((b, s, D_MODEL), F32),
                   jax.ShapeDtypeStruct((b, s, D_MODEL), BF16)],
        compiler_params=_params(2),
        name="merge",
    )(x, hf, hb, gg, sa, sb, o, mod, wr, wm, wo, g1, b1)


def _ffn_kernel(u_ref, up_ref, un_ref, x1_ref, mod_ref, wv_ref, wg_ref, cw_ref, cb_ref, wd_ref,
                g_ref, b_ref, y_ref, ue_ref, hv_ref, hg_ref, *, tm):
    i = pl.program_id(1)
    last = pl.num_programs(1) - 1
    halo = BF16_ROWS
    ue_ref[0:halo, :] = up_ref[...]
    ue_ref[halo:halo + tm, :] = u_ref[...]
    ue_ref[halo + tm:, :] = un_ref[...]
    ue = ue_ref[...]
    acc = jnp.zeros((tm, D_MODEL), F32)
    for j in range(N_FF_CHUNKS):
        sl = slice(j * FF_CHUNK, (j + 1) * FF_CHUNK)
        slg = slice(D_FF + j * FF_CHUNK, D_FF + (j + 1) * FF_CHUNK)
        hv_ref[...] = _dot(ue, wv_ref[:, sl])
        hg_ref[...] = _dot(ue, wg_ref[:, sl])

        @pl.when(i == 0)
        def _():
            hv_ref[0:halo, :] = jnp.zeros((halo, FF_CHUNK), F32)
            hg_ref[0:halo, :] = jnp.zeros((halo, FF_CHUNK), F32)

        @pl.when(i == last)
        def _():
            hv_ref[halo + tm:, :] = jnp.zeros((halo, FF_CHUNK), F32)
            hg_ref[halo + tm:, :] = jnp.zeros((halo, FF_CHUNK), F32)

        def conv(h_ref, w, bias):
            return (bias + w[0:1] * h_ref[pl.ds(halo - 1, tm), :]
                    + w[1:2] * h_ref[pl.ds(halo, tm), :]
                    + w[2:3] * h_ref[pl.ds(halo + 1, tm), :])

        cv = conv(hv_ref, cw_ref[:, sl], cb_ref[:, sl])
        cg = conv(hg_ref, cw_ref[:, slg], cb_ref[:, slg])
        act = (cg * jax.nn.sigmoid(cg) * cv).astype(BF16)
        acc = acc + _dot(act, wd_ref[sl, :])
    mod = mod_ref[...]
    y_ref[...] = _ln_plain(DN_ALPHA * x1_ref[...] + mod[5:6] * acc) * g_ref[...] + b_ref[...]


def _ffn(u2, x1, mod, wv, wg, cw, cb, wd, g2, b2, tm):
    b, s, _ = x1.shape
    hpb = tm // BF16_ROWS
    nhb = s // BF16_ROWS
    tok = pl.BlockSpec((None, tm, D_MODEL), lambda bi, i: (bi, i, 0))
    return pl.pallas_call(
        functools.partial(_ffn_kernel, tm=tm),
        grid=(b, s // tm),
        in_specs=[tok,
                  pl.BlockSpec((None, BF16_ROWS, D_MODEL),
                               lambda bi, i: (bi, jnp.maximum(i * hpb - 1, 0), 0)),
                  pl.BlockSpec((None, BF16_ROWS, D_MODEL),
                               lambda bi, i: (bi, jnp.minimum((i + 1) * hpb, nhb - 1), 0)),
                  tok,
                  pl.BlockSpec((None, 8, D_MODEL), lambda bi, i: (bi, 0, 0)),
                  _const_spec(wv.shape), _const_spec(wg.shape), _const_spec(cw.shape),
                  _const_spec(cb.shape), _const_spec(wd.shape), _const_spec(g2.shape),
                  _const_spec(b2.shape)],
        out_specs=tok,
        out_shape=jax.ShapeDtypeStruct((b, s, D_MODEL), F32),
        scratch_shapes=[pltpu.VMEM((tm + 2 * BF16_ROWS, D_MODEL), BF16),
                        pltpu.VMEM((tm + 2 * BF16_ROWS, FF_CHUNK), F32),
                        pltpu.VMEM((tm + 2 * BF16_ROWS, FF_CHUNK), F32)],
        compiler_params=_params(2),
        name="ffn",
    )(u2, u2, u2, x1, mod, wv, wg, cw, cb, wd, g2, b2)


def _rope_table(s):
    inv = 1.0 / (ROPE_BASE ** (jnp.arange(0, QK_ROPE, 2, dtype=F32) / QK_ROPE))
    ang = jnp.arange(s, dtype=F32)[:, None] * inv[None, :]
    cos, sin = jnp.cos(ang), jnp.sin(ang)
    one = jnp.ones((s, QK_NOPE), F32)
    zero = jnp.zeros((s, HALF_ROPE), F32)
    tail = jnp.zeros((s, HEAD_PAD - QK_NOPE - QK_ROPE), F32)
    zn = jnp.zeros((s, QK_NOPE), F32)
    cos_t = jnp.concatenate([one, cos, cos, tail], axis=1)
    sin_lo = jnp.concatenate([zn, -sin, zero, tail], axis=1)
    sin_hi = jnp.concatenate([zn, zero, sin, tail], axis=1)
    return jnp.stack([cos_t, sin_lo, sin_hi])


def _prep_weights(w_in, lru_w_r, lru_w_i, w_q_b, w_kv_b, w_up):
    o1, o2 = D_RNN, 2 * D_RNN
    o3, o4, o5 = o2 + Q_LORA, o2 + Q_LORA + KV_LORA, o2 + Q_LORA + KV_LORA + QK_ROPE
    o6 = o5 + D_MODEL
    kr = jnp.pad(w_in[:, o4:o5], ((0, 0), (QK_NOPE, HEAD_PAD - QK_NOPE - QK_ROPE)))
    w1 = jnp.concatenate([w_in[:, :o2], w_in[:, o5:o6], w_in[:, o6:], w_in[:, o2:o4], kr],
                         axis=1).astype(BF16)
    eye = jnp.eye(RNN_BLOCKS // N_GATE_GROUPS, dtype=F32)

    def blockdiag(w):
        wgp = w.reshape(2, N_GATE_GROUPS, RNN_BLOCKS // N_GATE_GROUPS, RNN_BLOCK, RNN_BLOCK)
        return jnp.einsum("dgnkj,nm->dgnkmj", wgp, eye).reshape(2, N_GATE_GROUPS, GATE_GROUP, GATE_GROUP)

    wgate = jnp.concatenate([blockdiag(lru_w_r), blockdiag(lru_w_i)], axis=-1).astype(BF16)
    qscale = (QK_NOPE + QK_ROPE) ** -0.5 * math.log2(math.e)
    wq = jnp.pad((w_q_b * qscale).reshape(Q_LORA, N_HEADS, QK_NOPE + QK_ROPE),
                 ((0, 0), (0, 0), (0, HEAD_PAD - QK_NOPE - QK_ROPE)))
    wq = wq.reshape(Q_LORA, N_HEADS * HEAD_PAD).astype(BF16)
    wkv = w_kv_b.reshape(KV_LORA, N_HEADS, QK_NOPE + V_HEAD)
    wk = jnp.pad(wkv[:, :, :QK_NOPE], ((0, 0), (0, 0), (0, HEAD_PAD - QK_NOPE)))
    wk = wk.reshape(KV_LORA, N_HEADS * HEAD_PAD).astype(BF16)
    wvt = wkv[:, :, QK_NOPE:].reshape(KV_LORA, N_HEADS * V_HEAD).T.astype(BF16)
    return w1, wgate, wq, wk, wvt, w_up[:, :D_FF].astype(BF16), w_up[:, D_FF:].astype(BF16)


def _tile(s, pref):
    return min(s, pref)


def _layer(x, mod, p, *, tm_in=256, ts=256, tq=256, tk=1024, tm_merge=256, tm_ffn=256):
    s = x.shape[1]
    rope_tab = _rope_table(s)
    xr, gg, sa, sb, q, k, vt = _inproj(x, mod, rope_tab, p["w1"], p["q_norm"], p["wq"], p["kv_norm"],
                                       p["wk"], p["wvt"], _tile(s, tm_in))
    hf, hb = _lru(xr, p["rnn_conv_w"], p["rnn_conv_b"], p["wgate"], p["lru_b_r"], p["lru_b_i"],
                  p["lru_lam"], _tile(s, ts))
    o = _attn(q, k, vt, _tile(s, tq), _tile(s // 4, tk))
    x1, u2 = _merge(x, hf, hb, gg, sa, sb, o, mod, p["w_rnn_proj"], p["w_mla_proj"], p["w_out"],
                    p["ln1_g"], p["ln1_b"], _tile(s, tm_merge))
    return _ffn(u2, x1, mod, p["w_up_v"], p["w_up_g"], p["ffn_conv_w"], p["ffn_conv_b"], p["w_down"],
                p["ln2_g"], p["ln2_b"], _tile(s, tm_ffn))


def kernel(x_prompt, x_sample, c_prompt, c_sample, w_ada, b_ada, w_in, rnn_conv_w, rnn_conv_b, lru_w_r, lru_b_r, lru_w_i, lru_b_i, lru_lam, w_rnn_proj, q_norm, w_q_b, kv_norm, w_kv_b, w_mla_proj, w_out, ln1_g, ln1_b, w_up, ffn_conv_w, ffn_conv_b, w_down, ln2_g, ln2_b):
    assert w_ada.shape[0] == DEPTH
    l = 0
    nb_p, nb_s = c_prompt.shape[0], c_sample.shape[0]
    rows = -(-(nb_p + nb_s) // SUBLANES) * SUBLANES
    c_all = jnp.concatenate([c_prompt, c_sample, jnp.zeros((rows - nb_p - nb_s, D_MODEL), F32)])
    mod = _ada(c_all, w_ada[l], b_ada[l][None, :]).reshape(rows, 6, D_MODEL)
    mod = jnp.pad(mod, ((0, 0), (0, 2), (0, 0)))
    w1, wgate, wq, wk, wvt, w_up_v, w_up_g = _prep_weights(w_in[l], lru_w_r[l], lru_w_i[l], w_q_b[l],
                                                         w_kv_b[l], w_up[l])
    row = lambda a: a[l][None, :]
    p = {
        "w1": w1, "wgate": wgate, "wq": wq, "wk": wk, "wvt": wvt, "w_up_v": w_up_v, "w_up_g": w_up_g,
        "q_norm": row(q_norm), "kv_norm": row(kv_norm),
        "rnn_conv_w": rnn_conv_w[l], "rnn_conv_b": row(rnn_conv_b),
        "lru_b_r": lru_b_r[l], "lru_b_i": lru_b_i[l], "lru_lam": lru_lam[l],
        "w_rnn_proj": w_rnn_proj[l].astype(BF16), "w_mla_proj": w_mla_proj[l].astype(BF16),
        "w_out": w_out[l].astype(BF16), "ln1_g": row(ln1_g), "ln1_b": row(ln1_b),
        "ffn_conv_w": ffn_conv_w[l], "ffn_conv_b": row(ffn_conv_b),
        "w_down": w_down[l].astype(BF16), "ln2_g": row(ln2_g), "ln2_b": row(ln2_b),
    }
    y_prompt = _layer(x_prompt, mod[:nb_p], p)
    y_sample = _layer(x_sample, mod[nb_p:nb_p + nb_s], p)
    return (y_prompt, y_sample)
```

```python
import functools
import math

import jax
import jax.numpy as jnp
from jax import lax
from jax.experimental import pallas as pl
from jax.experimental.pallas import tpu as pltpu

F32 = jnp.float32
BF16 = jnp.bfloat16

D_MODEL = 1024
D_RNN = 1024
RNN_BLOCKS = 16
RNN_BLOCK = D_RNN // RNN_BLOCKS
RNN_CONV = 4
RNN_CONV_LEFT = 2
LRU_C = 8.0
N_HEADS = 16
QK_NOPE = 64
QK_ROPE = 32
V_HEAD = 64
Q_LORA = 384
KV_LORA = 256
ROPE_BASE = 10000.0
D_FF = 2816
FFN_CONV = 3
LN_EPS = 1e-5
RMS_EPS = 1e-6
DEPTH = 1
DN_ALPHA = (2.0 * DEPTH) ** 0.25

LANES = 128
SUBLANES = 8
BF16_ROWS = 16
MXU_DIM = 256
VMEM_LIMIT = 56 * 1024 * 1024

HEAD_PAD = LANES
HALF_ROPE = QK_ROPE // 2
GATE_GROUP = MXU_DIM
N_GATE_GROUPS = D_RNN // GATE_GROUP
ATTN_UNROLL = 8
FF_CHUNK = MXU_DIM
N_FF_CHUNKS = D_FF // FF_CHUNK
OFF_QL = 4 * D_MODEL
OFF_KVL = OFF_QL + Q_LORA
OFF_KR = OFF_KVL + KV_LORA
N_IN_PAD = OFF_KR + HEAD_PAD


def _const_spec(shape):
    nd = len(shape)
    return pl.BlockSpec(shape, lambda *_: (0,) * nd, pipeline_mode=pl.Buffered(1))


def _params(n_grid):
    return pltpu.CompilerParams(dimension_semantics=("arbitrary",) * n_grid,
                                vmem_limit_bytes=VMEM_LIMIT)


def _ln_plain(x):
    mu = jnp.mean(x, axis=-1, keepdims=True)
    xc = x - mu
    var = jnp.mean(xc * xc, axis=-1, keepdims=True)
    return xc * lax.rsqrt(var + LN_EPS)


def _rms(x, g):
    return x * lax.rsqrt(jnp.mean(x * x, axis=-1, keepdims=True) + RMS_EPS) * g


def _dot(a, b):
    return jnp.dot(a, b, preferred_element_type=F32)


def _ada_kernel(c_ref, w_ref, b_ref, o_ref):
    c = c_ref[...]
    s = (c * jax.nn.sigmoid(c)).astype(BF16)
    o_ref[...] = _dot(s, w_ref[...].astype(BF16)) + b_ref[...]


def _ada(c_all, w_ada, b_ada, tn=512):
    rows, n = c_all.shape[0], w_ada.shape[1]
    return pl.pallas_call(
        _ada_kernel,
        grid=(n // tn,),
        in_specs=[pl.BlockSpec((rows, D_MODEL), lambda j: (0, 0)),
                  pl.BlockSpec((D_MODEL, tn), lambda j: (0, j)),
                  pl.BlockSpec((1, tn), lambda j: (0, j))],
        out_specs=pl.BlockSpec((rows, tn), lambda j: (0, j)),
        out_shape=jax.ShapeDtypeStruct((rows, n), F32),
        compiler_params=_params(1),
        name="ada",
    )(c_all, w_ada, b_ada)


def _inproj_kernel(x_ref, mod_ref, rope_ref, w1_ref, qn_ref, wq_ref, kvn_ref, wk_ref, wvt_ref,
                   xr_ref, gg_ref, sa_ref, sb_ref, q_ref, k_ref, vt_ref):
    x = x_ref[...]
    mod = mod_ref[...]
    ub = (_ln_plain(x) * (1.0 + mod[1:2]) + mod[0:1]).astype(BF16)

    def proj(lo, hi):
        return _dot(ub, w1_ref[:, lo:hi])

    xr_ref[...] = proj(0, D_MODEL)
    gg_ref[...] = jax.nn.gelu(proj(D_MODEL, 2 * D_MODEL)).astype(BF16)
    sa_ref[...] = jax.nn.sigmoid(proj(2 * D_MODEL, 3 * D_MODEL)).astype(BF16)
    sb_ref[...] = jax.nn.sigmoid(proj(3 * D_MODEL, 4 * D_MODEL)).astype(BF16)
    qn = _rms(proj(OFF_QL, OFF_KVL), qn_ref[...]).astype(BF16)
    kvn = _rms(proj(OFF_KVL, OFF_KR), kvn_ref[...]).astype(BF16)
    kr = proj(OFF_KR, N_IN_PAD)

    cos_t, sin_lo, sin_hi = rope_ref[0], rope_ref[1], rope_ref[2]

    def rope(t):
        return (t * cos_t + pltpu.roll(t, HEAD_PAD - HALF_ROPE, 1) * sin_lo
                + pltpu.roll(t, HALF_ROPE, 1) * sin_hi)

    kpe = rope(kr)
    qf = _dot(qn, wq_ref[...])
    kf = _dot(kvn, wk_ref[...])
    for h in range(N_HEADS):
        sl = slice(h * HEAD_PAD, (h + 1) * HEAD_PAD)
        q_ref[:, sl] = rope(qf[:, sl]).astype(BF16)
        k_ref[:, sl] = (kf[:, sl] + kpe).astype(BF16)
    vt_ref[...] = lax.dot_general(wvt_ref[...], kvn, (((1,), (1,)), ((), ())),
                                  preferred_element_type=F32).astype(BF16)


def _inproj(x, mod, rope_tab, w1, qn, wq, kvn, wk, wv, tm):
    b, s, _ = x.shape
    tok = lambda w: pl.BlockSpec((None, tm, w), lambda bi, i: (bi, i, 0))
    shp = lambda w, dt: jax.ShapeDtypeStruct((b, s, w), dt)
    return pl.pallas_call(
        _inproj_kernel,
        grid=(b, s // tm),
        in_specs=[tok(D_MODEL),
                  pl.BlockSpec((None, 8, D_MODEL), lambda bi, i: (bi, 0, 0)),
                  pl.BlockSpec((3, tm, HEAD_PAD), lambda bi, i: (0, i, 0)),
                  _const_spec(w1.shape), _const_spec(qn.shape), _const_spec(wq.shape),
                  _const_spec(kvn.shape), _const_spec(wk.shape), _const_spec(wv.shape)],
        out_specs=[tok(D_RNN), tok(D_RNN), tok(D_MODEL), tok(D_MODEL),
                   tok(N_HEADS * HEAD_PAD), tok(N_HEADS * HEAD_PAD),
                   pl.BlockSpec((None, N_HEADS * V_HEAD, tm), lambda bi, i: (bi, 0, i))],
        out_shape=[shp(D_RNN, F32), shp(D_RNN, BF16), shp(D_MODEL, BF16), shp(D_MODEL, BF16),
                   shp(N_HEADS * HEAD_PAD, BF16), shp(N_HEADS * HEAD_PAD, BF16),
                   jax.ShapeDtypeStruct((b, N_HEADS * V_HEAD, s), BF16)],
        compiler_params=_params(2),
        name="inproj",
    )(x, mod, rope_tab, w1, qn, wq, kvn, wk, wv)


def _lru_kernel(xf_ref, xfp_ref, xfn_ref, xb_ref, xbp_ref, xbn_ref,
                cw_ref, cb_ref, wg_ref, br_ref, bi_ref, lam_ref,
                hf_ref, hb_ref, xs_ref, a_ref, u_ref, carry_ref, *, ts):
    c = pl.program_id(1)
    last = pl.num_programs(1) - 1

    @pl.when(c == 0)
    def _():
        carry_ref[...] = jnp.zeros_like(carry_ref)

    lam = lam_ref[...]
    sp = jnp.maximum(-lam, 0.0) + jnp.log1p(jnp.exp(-jnp.abs(lam)))
    cw = cw_ref[...]
    row = lax.broadcasted_iota(jnp.int32, (SUBLANES, D_RNN), 0)
    n_grp = ts // SUBLANES

    def gates(d, cur_ref, prev_ref, next_ref, keep_prev, keep_next):
        xs_ref[0:SUBLANES, :] = prev_ref[...] * keep_prev
        xs_ref[SUBLANES:SUBLANES + ts, :] = cur_ref[...]
        xs_ref[SUBLANES + ts:, :] = next_ref[...] * keep_next
        xc = cb_ref[...] + cw[0:1] * xs_ref[pl.ds(SUBLANES - RNN_CONV_LEFT, ts), :]
        for k in range(1, RNN_CONV):
            xc = xc + cw[k:k + 1] * xs_ref[pl.ds(SUBLANES - RNN_CONV_LEFT + k, ts), :]
        xcb = xc.astype(BF16)
        for g in range(N_GATE_GROUPS):
            sl = slice(g * GATE_GROUP, (g + 1) * GATE_GROUP)
            pre = _dot(xcb[:, sl], wg_ref[d, g])
            r = jax.nn.sigmoid(pre[:, :GATE_GROUP] + br_ref[d:d + 1, sl])
            i = jax.nn.sigmoid(pre[:, GATE_GROUP:] + bi_ref[d:d + 1, sl])
            log_a = (-LRU_C) * r * sp[d:d + 1, sl]
            a = jnp.exp(log_a)
            a_ref[d, :, sl] = a
            u_ref[d, :, sl] = jnp.sqrt(-jnp.tanh(log_a) * (a * a + 1.0)) * i * xc[:, sl]

    def scan(d, out_ref):
        def body(j, carry):
            g = j if d == 0 else n_grp - 1 - j
            off = pl.multiple_of(g * SUBLANES, SUBLANES)
            a8 = a_ref[d, pl.ds(off, SUBLANES), :]
            u8 = u_ref[d, pl.ds(off, SUBLANES), :]
            for sh in (1, 2, 4):
                if d == 0:
                    edge = row < sh
                    shift = sh
                else:
                    edge = row >= SUBLANES - sh
                    shift = SUBLANES - sh
                a_s = jnp.where(edge, 1.0, pltpu.roll(a8, shift, 0))
                u_s = jnp.where(edge, 0.0, pltpu.roll(u8, shift, 0))
                u8 = a8 * u_s + u8
                a8 = a8 * a_s
            h8 = u8 + a8 * carry
            out_ref[pl.ds(off, SUBLANES), :] = h8
            return h8[SUBLANES - 1:SUBLANES, :] if d == 0 else h8[0:1, :]

        carry_ref[d:d + 1, :] = lax.fori_loop(0, n_grp, body, carry_ref[d:d + 1, :])

    not_first = (c > 0).astype(F32)
    not_last = (c < last).astype(F32)
    gates(0, xf_ref, xfp_ref, xfn_ref, not_first, not_last)
    gates(1, xb_ref, xbp_ref, xbn_ref, not_last, not_first)
    scan(0, hf_ref)
    scan(1, hb_ref)


def _lru(xr, cw, cb, wg, br, bi, lam, ts):
    b, s, _ = xr.shape
    nc = s // ts
    hpb = ts // SUBLANES
    nhb = s // SUBLANES
    cur = lambda f: pl.BlockSpec((None, ts, D_RNN), lambda bi_, c: (bi_, f(c), 0))
    halo = lambda f: pl.BlockSpec((None, SUBLANES, D_RNN), lambda bi_, c: (bi_, f(c), 0))
    fwd = lambda c: c
    bwd = lambda c: nc - 1 - c
    prev = lambda f: (lambda c: jnp.maximum(f(c) * hpb - 1, 0))
    nxt = lambda f: (lambda c: jnp.minimum((f(c) + 1) * hpb, nhb - 1))
    return pl.pallas_call(
        functools.partial(_lru_kernel, ts=ts),
        grid=(b, nc),
        in_specs=[cur(fwd), halo(prev(fwd)), halo(nxt(fwd)),
                  cur(bwd), halo(prev(bwd)), halo(nxt(bwd)),
                  _const_spec(cw.shape), _const_spec(cb.shape), _const_spec(wg.shape),
                  _const_spec(br.shape), _const_spec(bi.shape), _const_spec(lam.shape)],
        out_specs=[cur(fwd), cur(bwd)],
        out_shape=[jax.ShapeDtypeStruct((b, s, D_RNN), F32)] * 2,
        scratch_shapes=[pltpu.VMEM((ts + 2 * SUBLANES, D_RNN), F32),
                        pltpu.VMEM((2, ts, D_RNN), F32),
                        pltpu.VMEM((2, ts, D_RNN), F32),
                        pltpu.VMEM((SUBLANES, D_RNN), F32)],
        compiler_params=_params(2),
        name="lru",
    )(xr, xr, xr, xr, xr, xr, cw, cb, wg, br, bi, lam)


def _attn_kernel(q_ref, k_ref, vt_ref, o_ref, st_ref, *, tq, tk):
    s_len = k_ref.shape[0]
    n_kv, n_q = s_len // tk, s_len // tq
    unroll = min(ATTN_UNROLL, n_kv)
    per_tile = n_kv // unroll
    assert n_kv % unroll == 0 and per_tile & (per_tile - 1) == 0 and n_kv & (n_kv - 1) == 0
    total = n_q * n_kv
    heads = [slice(hh * HEAD_PAD, (hh + 1) * HEAD_PAD) for hh in range(2)]

    def scores(t, hh, slot):
        t = jnp.minimum(t, total - 1)
        i = lax.shift_right_logical(t, n_kv.bit_length() - 1)
        j = jnp.bitwise_and(t, n_kv - 1)
        q = q_ref[pl.ds(pl.multiple_of(i * tq, tq), tq), heads[hh]]
        k = k_ref[pl.ds(pl.multiple_of(j * tk, tk), tk), heads[hh]]
        st = lax.dot_general(k, q, (((1,), (1,)), ((), ())), preferred_element_type=F32)
        st_ref[slot, hh] = st
        return jnp.max(st, axis=0, keepdims=True)

    ones = jnp.ones((BF16_ROWS, MXU_DIM), BF16)

    def softmax_pv(j, hh, slot, m_old, acc, cmax):
        off = pl.multiple_of(j * tk, tk)
        m_new = jnp.maximum(m_old, cmax)
        acc = jnp.exp2(m_old - m_new) * acc
        for t in range(tk // MXU_DIM):
            rows = slice(t * MXU_DIM, (t + 1) * MXU_DIM)
            pt = jnp.exp2(st_ref[slot, hh, rows, :] - m_new).astype(BF16)
            vt = vt_ref[hh * V_HEAD:(hh + 1) * V_HEAD, pl.ds(off + t * MXU_DIM, MXU_DIM)]
            acc = acc + _dot(jnp.concatenate([vt, ones], axis=0), pt)
        return m_new, acc

    def finish(i, carry):
        ot = jnp.concatenate([acc[:V_HEAD] / acc[V_HEAD:V_HEAD + 1] for _, acc, _ in carry], axis=0)
        o_ref[pl.ds(pl.multiple_of(i * tq, tq), tq), :] = ot.T.astype(o_ref.dtype)

    def body(bb, carry):
        i = lax.shift_right_logical(bb, per_tile.bit_length() - 1)
        jb = jnp.bitwise_and(bb, per_tile - 1)
        for step in range(unroll):
            slot = step % 2
            new = []
            for hh in range(2):
                m_old, acc, cmax = carry[hh]
                cmax_next = scores(bb * unroll + step + 1, hh, 1 - slot)
                new.append(softmax_pv(jb * unroll + step, hh, slot, m_old, acc, cmax) + (cmax_next,))
            carry = tuple(new)
        if per_tile == 1:
            finish(i, carry)
            return tuple((m0, acc0, cmax) for _, _, cmax in carry)
        tile_done = jb == per_tile - 1
        pl.when(tile_done)(lambda: finish(i, carry))
        return tuple((jnp.where(tile_done, m0, m), jnp.where(tile_done, acc0, acc), cmax)
                     for m, acc, cmax in carry)

    m0 = jnp.full((1, tq), -jnp.inf, F32)
    acc0 = jnp.zeros((V_HEAD + BF16_ROWS, tq), F32)
    lax.fori_loop(0, n_q * per_tile, body, tuple((m0, acc0, scores(0, hh, 0)) for hh in range(2)))


def _attn(q, k, vt, tq, tk):
    b, s, _ = q.shape
    resident = s * (4 * HEAD_PAD + 2 * V_HEAD) * 2
    mode = {} if 2 * resident <= VMEM_LIMIT // 4 else {"pipeline_mode": pl.Buffered(1)}
    return pl.pallas_call(
        functools.partial(_attn_kernel, tq=tq, tk=tk),
        scratch_shapes=[pltpu.VMEM((2, 2, tk, tq), F32)],
        grid=(b, N_HEADS // 2),
        in_specs=[pl.BlockSpec((None, s, 2 * HEAD_PAD), lambda bi, p: (bi, 0, p), **mode),
                  pl.BlockSpec((None, s, 2 * HEAD_PAD), lambda bi, p: (bi, 0, p), **mode),
                  pl.BlockSpec((None, 2 * V_HEAD, s), lambda bi, p: (bi, p, 0), **mode)],
        out_specs=pl.BlockSpec((None, s, 2 * V_HEAD), lambda bi, p: (bi, 0, p)),
        out_shape=jax.ShapeDtypeStruct((b, s, N_HEADS * V_HEAD), BF16),
        compiler_params=_params(2),
        name="attn",
    )(q, k, vt)


def _merge_kernel(x_ref, hf_ref, hb_ref, gg_ref, sa_ref, sb_ref, o_ref, mod_ref,
                  wr_ref, wm_ref, wo_ref, g_ref, b_ref, x1_ref, u2_ref):
    mod = mod_ref[...]
    ya = _dot((gg_ref[...] * (hf_ref[...] + hb_ref[...])).astype(BF16), wr_ref[...])
    yb = _dot(o_ref[...], wm_ref[...])
    mix = _dot((sa_ref[...] * ya + sb_ref[...] * yb).astype(BF16), wo_ref[...])
    x1 = _ln_plain(DN_ALPHA * x_ref[...] + mod[2:3] * mix) * g_ref[...] + b_ref[...]
    x1_ref[...] = x1
    u2_ref[...] = (_ln_plain(x1) * (1.0 + mod[4:5]) + mod[3:4]).astype(BF16)


def _merge(x, hf, hb, gg, sa, sb, o, mod, wr, wm, wo, g1, b1, tm):
    b, s, _ = x.shape
    tok = pl.BlockSpec((None, tm, D_MODEL), lambda bi, i: (bi, i, 0))
    return pl.pallas_call(
        _merge_kernel,
        grid=(b, s // tm),
        in_specs=[tok] * 7 + [pl.BlockSpec((None, 8, D_MODEL), lambda bi, i: (bi, 0, 0)),
                              _const_spec(wr.shape), _const_spec(wm.shape), _const_spec(wo.shape),
                              _const_spec(g1.shape), _const_spec(b1.shape)],
        out_specs=[tok, tok],
        out_shape=[jax.ShapeDtypeStruct((b, s, D_MODEL), F32),
                   jax.ShapeDtypeStruct((b, s, D_MODEL), BF16)],
        compiler_params=_params(2),
        name="merge",
    )(x, hf, hb, gg, sa, sb, o, mod, wr, wm, wo, g1, b1)


def _ffn_kernel(u_ref, up_ref, un_ref, x1_ref, mod_ref, wv_ref, wg_ref, cw_ref, cb_ref, wd_ref,
                g_ref, b_ref, y_ref, ue_ref, hv_ref, hg_ref, *, tm):
    i = pl.program_id(1)
    last = pl.num_programs(1) - 1
    halo = BF16_ROWS
    keep_prev = (i > 0).astype(BF16)
    keep_next = (i < last).astype(BF16)
    ue_ref[0:halo, :] = up_ref[...] * keep_prev
    ue_ref[halo:halo + tm, :] = u_ref[...]
    ue_ref[halo + tm:, :] = un_ref[...] * keep_next
    ue = ue_ref[...]

    def up(j):
        sl = slice(j * FF_CHUNK, (j + 1) * FF_CHUNK)
        hv_ref[j % 2] = _dot(ue, wv_ref[:, sl])
        hg_ref[j % 2] = _dot(ue, wg_ref[:, sl])

    def conv(h_ref, slot, w, bias):
        return (bias + w[0:1] * h_ref[slot, pl.ds(halo - 1, tm), :]
                + w[1:2] * h_ref[slot, pl.ds(halo, tm), :]
                + w[2:3] * h_ref[slot, pl.ds(halo + 1, tm), :])

    up(0)
    acc = jnp.zeros((tm, D_MODEL), F32)
    for j in range(N_FF_CHUNKS):
        if j + 1 < N_FF_CHUNKS:
            up(j + 1)
        sl = slice(j * FF_CHUNK, (j + 1) * FF_CHUNK)
        slg = slice(D_FF + j * FF_CHUNK, D_FF + (j + 1) * FF_CHUNK)
        cv = conv(hv_ref, j % 2, cw_ref[:, sl], cb_ref[:, sl])
        cg = conv(hg_ref, j % 2, cw_ref[:, slg], cb_ref[:, slg])
        act = (cg * jax.nn.sigmoid(cg) * cv).astype(BF16)
        acc = acc + _dot(act, wd_ref[sl, :])
    mod = mod_ref[...]
    y_ref[...] = _ln_plain(DN_ALPHA * x1_ref[...] + mod[5:6] * acc) * g_ref[...] + b_ref[...]


def _ffn(u2, x1, mod, wv, wg, cw, cb, wd, g2, b2, tm):
    b, s, _ = x1.shape
    hpb = tm // BF16_ROWS
    nhb = s // BF16_ROWS
    tok = pl.BlockSpec((None, tm, D_MODEL), lambda bi, i: (bi, i, 0))
    return pl.pallas_call(
        functools.partial(_ffn_kernel, tm=tm),
        grid=(b, s // tm),
        in_specs=[tok,
                  pl.BlockSpec((None, BF16_ROWS, D_MODEL),
                               lambda bi, i: (bi, jnp.maximum(i * hpb - 1, 0), 0)),
                  pl.BlockSpec((None, BF16_ROWS, D_MODEL),
                               lambda bi, i: (bi, jnp.minimum((i + 1) * hpb, nhb - 1), 0)),
                  tok,
                  pl.BlockSpec((None, 8, D_MODEL), lambda bi, i: (bi, 0, 0)),
                  _const_spec(wv.shape), _const_spec(wg.shape), _const_spec(cw.shape),
                  _const_spec(cb.shape), _const_spec(wd.shape), _const_spec(g2.shape),
                  _const_spec(b2.shape)],
        out_specs=tok,
        out_shape=jax.ShapeDtypeStruct((b, s, D_MODEL), F32),
        scratch_shapes=[pltpu.VMEM((tm + 2 * BF16_ROWS, D_MODEL), BF16),
                        pltpu.VMEM((2, tm + 2 * BF16_ROWS, FF_CHUNK), F32),
                        pltpu.VMEM((2, tm + 2 * BF16_ROWS, FF_CHUNK), F32)],
        compiler_params=_params(2),
        name="ffn",
    )(u2, u2, u2, x1, mod, wv, wg, cw, cb, wd, g2, b2)


def _rope_table(s):
    inv = 1.0 / (ROPE_BASE ** (jnp.arange(0, QK_ROPE, 2, dtype=F32) / QK_ROPE))
    ang = jnp.arange(s, dtype=F32)[:, None] * inv[None, :]
    cos, sin = jnp.cos(ang), jnp.sin(ang)
    one = jnp.ones((s, QK_NOPE), F32)
    zero = jnp.zeros((s, HALF_ROPE), F32)
    tail = jnp.zeros((s, HEAD_PAD - QK_NOPE - QK_ROPE), F32)
    zn = jnp.zeros((s, QK_NOPE), F32)
    cos_t = jnp.concatenate([one, cos, cos, tail], axis=1)
    sin_lo = jnp.concatenate([zn, -sin, zero, tail], axis=1)
    sin_hi = jnp.concatenate([zn, zero, sin, tail], axis=1)
    return jnp.stack([cos_t, sin_lo, sin_hi])


def _prep_weights(w_in, lru_w_r, lru_w_i, w_q_b, w_kv_b, w_up):
    o1, o2 = D_RNN, 2 * D_RNN
    o3, o4, o5 = o2 + Q_LORA, o2 + Q_LORA + KV_LORA, o2 + Q_LORA + KV_LORA + QK_ROPE
    o6 = o5 + D_MODEL
    kr = jnp.pad(w_in[:, o4:o5], ((0, 0), (QK_NOPE, HEAD_PAD - QK_NOPE - QK_ROPE)))
    w1 = jnp.concatenate([w_in[:, :o2], w_in[:, o5:o6], w_in[:, o6:], w_in[:, o2:o4], kr],
                         axis=1).astype(BF16)
    eye = jnp.eye(RNN_BLOCKS // N_GATE_GROUPS, dtype=F32)

    def blockdiag(w):
        wgp = w.reshape(2, N_GATE_GROUPS, RNN_BLOCKS // N_GATE_GROUPS, RNN_BLOCK, RNN_BLOCK)
        return jnp.einsum("dgnkj,nm->dgnkmj", wgp, eye).reshape(2, N_GATE_GROUPS, GATE_GROUP, GATE_GROUP)

    wgate = jnp.concatenate([blockdiag(lru_w_r), blockdiag(lru_w_i)], axis=-1).astype(BF16)
    qscale = (QK_NOPE + QK_ROPE) ** -0.5 * math.log2(math.e)
    wq = jnp.pad((w_q_b * qscale).reshape(Q_LORA, N_HEADS, QK_NOPE + QK_ROPE),
                 ((0, 0), (0, 0), (0, HEAD_PAD - QK_NOPE - QK_ROPE)))
    wq = wq.reshape(Q_LORA, N_HEADS * HEAD_PAD).astype(BF16)
    wkv = w_kv_b.reshape(KV_LORA, N_HEADS, QK_NOPE + V_HEAD)
    wk = jnp.pad(wkv[:, :, :QK_NOPE], ((0, 0), (0, 0), (0, HEAD_PAD - QK_NOPE)))
    wk = wk.reshape(KV_LORA, N_HEADS * HEAD_PAD).astype(BF16)
    wvt = wkv[:, :, QK_NOPE:].reshape(KV_LORA, N_HEADS * V_HEAD).T.astype(BF16)
    return w1, wgate, wq, wk, wvt, w_up[:, :D_FF].astype(BF16), w_up[:, D_FF:].astype(BF16)


def _tile(s, pref):
    return min(s, pref)


def _layer(x, mod, p, *, tm_in=256, ts=256, tq=256, tk=1024, tm_merge=256, tm_ffn=512):
    s = x.shape[1]
    rope_tab = _rope_table(s)
    xr, gg, sa, sb, q, k, vt = _inproj(x, mod, rope_tab, p["w1"], p["q_norm"], p["wq"], p["kv_norm"],
                                       p["wk"], p["wvt"], _tile(s, tm_in))
    hf, hb = _lru(xr, p["rnn_conv_w"], p["rnn_conv_b"], p["wgate"], p["lru_b_r"], p["lru_b_i"],
                  p["lru_lam"], _tile(s, ts))
    o = _attn(q, k, vt, _tile(s, tq), _tile(s // 4, tk))
    x1, u2 = _merge(x, hf, hb, gg, sa, sb, o, mod, p["w_rnn_proj"], p["w_mla_proj"], p["w_out"],
                    p["ln1_g"], p["ln1_b"], _tile(s, tm_merge))
    return _ffn(u2, x1, mod, p["w_up_v"], p["w_up_g"], p["ffn_conv_w"], p["ffn_conv_b"], p["w_down"],
                p["ln2_g"], p["ln2_b"], _tile(s, tm_ffn))


def kernel(x_prompt, x_sample, c_prompt, c_sample, w_ada, b_ada, w_in, rnn_conv_w, rnn_conv_b, lru_w_r, lru_b_r, lru_w_i, lru_b_i, lru_lam, w_rnn_proj, q_norm, w_q_b, kv_norm, w_kv_b, w_mla_proj, w_out, ln1_g, ln1_b, w_up, ffn_conv_w, ffn_conv_b, w_down, ln2_g, ln2_b):
    assert w_ada.shape[0] == DEPTH
    l = 0
    nb_p, nb_s = c_prompt.shape[0], c_sample.shape[0]
    rows = -(-(nb_p + nb_s) // SUBLANES) * SUBLANES
    c_all = jnp.concatenate([c_prompt, c_sample, jnp.zeros((rows - nb_p - nb_s, D_MODEL), F32)])
    mod = _ada(c_all, w_ada[l], b_ada[l][None, :]).reshape(rows, 6, D_MODEL)
    mod = jnp.pad(mod, ((0, 0), (0, 2), (0, 0)))
    w1, wgate, wq, wk, wvt, w_up_v, w_up_g = _prep_weights(w_in[l], lru_w_r[l], lru_w_i[l], w_q_b[l],
                                                         w_kv_b[l], w_up[l])
    row = lambda a: a[l][None, :]
    p = {
        "w1": w1, "wgate": wgate, "wq": wq, "wk": wk, "wvt": wvt, "w_up_v": w_up_v, "w_up_g": w_up_g,
        "q_norm": row(q_norm), "kv_norm": row(kv_norm),
        "rnn_conv_w": rnn_conv_w[l], "rnn_conv_b": row(rnn_conv_b),
        "lru_b_r": lru_b_r[l], "lru_b_i": lru_b_i[l], "lru_lam": lru_lam[l],
        "w_rnn_proj": w_rnn_proj[l].astype(BF16), "w_mla_proj": w_mla_proj[l].astype(BF16),
        "w_out": w_out[l].astype(BF16), "ln1_g": row(ln1_g), "ln1_b": row(ln1_b),
        "ffn_conv_w": ffn_conv_w[l], "ffn_conv_b": row(ffn_conv_b),
        "w_down": w_down[l].astype(BF16), "ln2_g": row(ln2_g), "ln2_b": row(ln2_b),
    }
    y_prompt = _layer(x_prompt, mod[:nb_p], p)
    y_sample = _layer(x_sample, mod[nb_p:nb_p + nb_s], p)
    return (y_prompt, y_sample)
```

```python
import functools
import math

import jax
import jax.numpy as jnp
from jax import lax
from jax.experimental import pallas as pl
from jax.experimental.pallas import tpu as pltpu

F32 = jnp.float32
BF16 = jnp.bfloat16

D_MODEL = 1024
D_RNN = 1024
RNN_BLOCKS = 16
RNN_BLOCK = D_RNN // RNN_BLOCKS
RNN_CONV = 4
RNN_CONV_LEFT = 2
LRU_C = 8.0
N_HEADS = 16
QK_NOPE = 64
QK_ROPE = 32
V_HEAD = 64
Q_LORA = 384
KV_LORA = 256
ROPE_BASE = 10000.0
D_FF = 2816
FFN_CONV = 3
LN_EPS = 1e-5
RMS_EPS = 1e-6
DEPTH = 1
DN_ALPHA = (2.0 * DEPTH) ** 0.25

LANES = 128
SUBLANES = 8
BF16_ROWS = 16
MXU_DIM = 256
VMEM_LIMIT = 56 * 1024 * 1024

HEAD_PAD = LANES
HALF_ROPE = QK_ROPE // 2
GATE_GROUP = MXU_DIM
N_GATE_GROUPS = D_RNN // GATE_GROUP
ATTN_UNROLL = 8
FF_CHUNK = MXU_DIM
N_FF_CHUNKS = D_FF // FF_CHUNK
OFF_QL = 4 * D_MODEL
OFF_KVL = OFF_QL + Q_LORA
OFF_KR = OFF_KVL + KV_LORA
N_IN_PAD = OFF_KR + HEAD_PAD


def _const_spec(shape):
    nd = len(shape)
    return pl.BlockSpec(shape, lambda *_: (0,) * nd, pipeline_mode=pl.Buffered(1))


def _params(n_grid):
    return pltpu.CompilerParams(dimension_semantics=("arbitrary",) * n_grid,
                                vmem_limit_bytes=VMEM_LIMIT)


def _ln_plain(x):
    mu = jnp.mean(x, axis=-1, keepdims=True)
    xc = x - mu
    var = jnp.mean(xc * xc, axis=-1, keepdims=True)
    return xc * lax.rsqrt(var + LN_EPS)


def _rms(x, g):
    return x * lax.rsqrt(jnp.mean(x * x, axis=-1, keepdims=True) + RMS_EPS) * g


def _dot(a, b):
    return jnp.dot(a, b, preferred_element_type=F32)


def _ada_kernel(c_ref, w_ref, b_ref, o_ref):
    c = c_ref[...]
    s = (c * jax.nn.sigmoid(c)).astype(BF16)
    o_ref[...] = _dot(s, w_ref[...].astype(BF16)) + b_ref[...]


def _ada(c_all, w_ada, b_ada, tn=512):
    rows, n = c_all.shape[0], w_ada.shape[1]
    return pl.pallas_call(
        _ada_kernel,
        grid=(n // tn,),
        in_specs=[pl.BlockSpec((rows, D_MODEL), lambda j: (0, 0)),
                  pl.BlockSpec((D_MODEL, tn), lambda j: (0, j)),
                  pl.BlockSpec((1, tn), lambda j: (0, j))],
        out_specs=pl.BlockSpec((rows, tn), lambda j: (0, j)),
        out_shape=jax.ShapeDtypeStruct((rows, n), F32),
        compiler_params=_params(1),
        name="ada",
    )(c_all, w_ada, b_ada)


def _inproj_kernel(x_ref, mod_ref, rope_ref, w1_ref, qn_ref, wq_ref, kvn_ref, wk_ref, wvt_ref,
                   xr_ref, gg_ref, sa_ref, sb_ref, q_ref, k_ref, vt_ref):
    x = x_ref[...]
    mod = mod_ref[...]
    ub = (_ln_plain(x) * (1.0 + mod[1:2]) + mod[0:1]).astype(BF16)

    def proj(lo, hi):
        return _dot(ub, w1_ref[:, lo:hi])

    xr_ref[...] = proj(0, D_MODEL)
    gg_ref[...] = jax.nn.gelu(proj(D_MODEL, 2 * D_MODEL)).astype(BF16)
    sa_ref[...] = jax.nn.sigmoid(proj(2 * D_MODEL, 3 * D_MODEL)).astype(BF16)
    sb_ref[...] = jax.nn.sigmoid(proj(3 * D_MODEL, 4 * D_MODEL)).astype(BF16)
    qn = _rms(proj(OFF_QL, OFF_KVL), qn_ref[...]).astype(BF16)
    kvn = _rms(proj(OFF_KVL, OFF_KR), kvn_ref[...]).astype(BF16)
    kr = proj(OFF_KR, N_IN_PAD)

    cos_t, sin_lo, sin_hi = rope_ref[0], rope_ref[1], rope_ref[2]

    def rope(t):
        return (t * cos_t + pltpu.roll(t, HEAD_PAD - HALF_ROPE, 1) * sin_lo
                + pltpu.roll(t, HALF_ROPE, 1) * sin_hi)

    kpe = rope(kr)
    qf = _dot(qn, wq_ref[...])
    kf = _dot(kvn, wk_ref[...])
    for h in range(N_HEADS):
        sl = slice(h * HEAD_PAD, (h + 1) * HEAD_PAD)
        q_ref[:, sl] = rope(qf[:, sl]).astype(BF16)
        k_ref[:, sl] = (kf[:, sl] + kpe).astype(BF16)
    vt_ref[...] = lax.dot_general(wvt_ref[...], kvn, (((1,), (1,)), ((), ())),
                                  preferred_element_type=F32).astype(BF16)


def _inproj(x, mod, rope_tab, w1, qn, wq, kvn, wk, wv, tm):
    b, s, _ = x.shape
    tok = lambda w: pl.BlockSpec((None, tm, w), lambda bi, i: (bi, i, 0))
    shp = lambda w, dt: jax.ShapeDtypeStruct((b, s, w), dt)
    return pl.pallas_call(
        _inproj_kernel,
        grid=(b, s // tm),
        in_specs=[tok(D_MODEL),
                  pl.BlockSpec((None, 8, D_MODEL), lambda bi, i: (bi, 0, 0)),
                  pl.BlockSpec((3, tm, HEAD_PAD), lambda bi, i: (0, i, 0)),
                  _const_spec(w1.shape), _const_spec(qn.shape), _const_spec(wq.shape),
                  _const_spec(kvn.shape), _const_spec(wk.shape), _const_spec(wv.shape)],
        out_specs=[tok(D_RNN), tok(D_RNN), tok(D_MODEL), tok(D_MODEL),
                   tok(N_HEADS * HEAD_PAD), tok(N_HEADS * HEAD_PAD),
                   pl.BlockSpec((None, N_HEADS * V_HEAD, tm), lambda bi, i: (bi, 0, i))],
        out_shape=[shp(D_RNN, F32), shp(D_RNN, BF16), shp(D_MODEL, BF16), shp(D_MODEL, BF16),
                   shp(N_HEADS * HEAD_PAD, BF16), shp(N_HEADS * HEAD_PAD, BF16),
                   jax.ShapeDtypeStruct((b, N_HEADS * V_HEAD, s), BF16)],
        compiler_params=_params(2),
        name="inproj",
    )(x, mod, rope_tab, w1, qn, wq, kvn, wk, wv)


def _lru_kernel(xf_ref, xfp_ref, xfn_ref, xb_ref, xbp_ref, xbn_ref,
                cw_ref, cb_ref, wg_ref, br_ref, bi_ref, lam_ref,
                hf_ref, hb_ref, xs_ref, a_ref, u_ref, carry_ref, *, ts):
    c = pl.program_id(1)
    last = pl.num_programs(1) - 1

    @pl.when(c == 0)
    def _():
        carry_ref[...] = jnp.zeros_like(carry_ref)

    lam = lam_ref[...]
    sp = jnp.maximum(-lam, 0.0) + jnp.log1p(jnp.exp(-jnp.abs(lam)))
    cw = cw_ref[...]
    row = lax.broadcasted_iota(jnp.int32, (SUBLANES, D_RNN), 0)
    n_grp = ts // SUBLANES

    def gates(d, cur_ref, prev_ref, next_ref, keep_prev, keep_next):
        xs_ref[0:SUBLANES, :] = prev_ref[...] * keep_prev
        xs_ref[SUBLANES:SUBLANES + ts, :] = cur_ref[...]
        xs_ref[SUBLANES + ts:, :] = next_ref[...] * keep_next
        xc = cb_ref[...] + cw[0:1] * xs_ref[pl.ds(SUBLANES - RNN_CONV_LEFT, ts), :]
        for k in range(1, RNN_CONV):
            xc = xc + cw[k:k + 1] * xs_ref[pl.ds(SUBLANES - RNN_CONV_LEFT + k, ts), :]
        xcb = xc.astype(BF16)
        for g in range(N_GATE_GROUPS):
            sl = slice(g * GATE_GROUP, (g + 1) * GATE_GROUP)
            pre = _dot(xcb[:, sl], wg_ref[d, g])
            r = jax.nn.sigmoid(pre[:, :GATE_GROUP] + br_ref[d:d + 1, sl])
            i = jax.nn.sigmoid(pre[:, GATE_GROUP:] + bi_ref[d:d + 1, sl])
            log_a = (-LRU_C) * r * sp[d:d + 1, sl]
            a = jnp.exp(log_a)
            a_ref[d, :, sl] = a
            u_ref[d, :, sl] = jnp.sqrt(-jnp.tanh(log_a) * (a * a + 1.0)) * i * xc[:, sl]

    def scan(d, out_ref):
        def body(j, carry):
            g = j if d == 0 else n_grp - 1 - j
            off = pl.multiple_of(g * SUBLANES, SUBLANES)
            a8 = a_ref[d, pl.ds(off, SUBLANES), :]
            u8 = u_ref[d, pl.ds(off, SUBLANES), :]
            for sh in (1, 2, 4):
                if d == 0:
                    edge = row < sh
                    shift = sh
                else:
                    edge = row >= SUBLANES - sh
                    shift = SUBLANES - sh
                a_s = jnp.where(edge, 1.0, pltpu.roll(a8, shift, 0))
                u_s = jnp.where(edge, 0.0, pltpu.roll(u8, shift, 0))
                u8 = a8 * u_s + u8
                a8 = a8 * a_s
            h8 = u8 + a8 * carry
            out_ref[pl.ds(off, SUBLANES), :] = h8
            return h8[SUBLANES - 1:SUBLANES, :] if d == 0 else h8[0:1, :]

        carry_ref[d:d + 1, :] = lax.fori_loop(0, n_grp, body, carry_ref[d:d + 1, :])

    not_first = (c > 0).astype(F32)
    not_last = (c < last).astype(F32)
    gates(0, xf_ref, xfp_ref, xfn_ref, not_first, not_last)
    gates(1, xb_ref, xbp_ref, xbn_ref, not_last, not_first)
    scan(0, hf_ref)
    scan(1, hb_ref)


def _lru(xr, cw, cb, wg, br, bi, lam, ts):
    b, s, _ = xr.shape
    nc = s // ts
    hpb = ts // SUBLANES
    nhb = s // SUBLANES
    cur = lambda f: pl.BlockSpec((None, ts, D_RNN), lambda bi_, c: (bi_, f(c), 0))
    halo = lambda f: pl.BlockSpec((None, SUBLANES, D_RNN), lambda bi_, c: (bi_, f(c), 0))
    fwd = lambda c: c
    bwd = lambda c: nc - 1 - c
    prev = lambda f: (lambda c: jnp.maximum(f(c) * hpb - 1, 0))
    nxt = lambda f: (lambda c: jnp.minimum((f(c) + 1) * hpb, nhb - 1))
    return pl.pallas_call(
        functools.partial(_lru_kernel, ts=ts),
        grid=(b, nc),
        in_specs=[cur(fwd), halo(prev(fwd)), halo(nxt(fwd)),
                  cur(bwd), halo(prev(bwd)), halo(nxt(bwd)),
                  _const_spec(cw.shape), _const_spec(cb.shape), _const_spec(wg.shape),
                  _const_spec(br.shape), _const_spec(bi.shape), _const_spec(lam.shape)],
        out_specs=[cur(fwd), cur(bwd)],
        out_shape=[jax.ShapeDtypeStruct((b, s, D_RNN), F32)] * 2,
        scratch_shapes=[pltpu.VMEM((ts + 2 * SUBLANES, D_RNN), F32),
                        pltpu.VMEM((2, ts, D_RNN), F32),
                        pltpu.VMEM((2, ts, D_RNN), F32),
                        pltpu.VMEM((SUBLANES, D_RNN), F32)],
        compiler_params=_params(2),
        name="lru",
    )(xr, xr, xr, xr, xr, xr, cw, cb, wg, br, bi, lam)


def _attn_kernel(q_ref, k_ref, vt_ref, o_ref, st_ref, *, tq, tk):
    s_len = k_ref.shape[0]
    n_kv, n_q = s_len // tk, s_len // tq
    total = n_q * n_kv
    unroll = min(ATTN_UNROLL, total)
    assert unroll % 2 == 0 and total % unroll == 0 and n_kv & (n_kv - 1) == 0
    per_tile = max(n_kv // unroll, 1)
    tiles_per_body = max(unroll // n_kv, 1)
    assert per_tile & (per_tile - 1) == 0
    heads = [slice(hh * HEAD_PAD, (hh + 1) * HEAD_PAD) for hh in range(2)]

    def scores(t, hh, slot):
        t = jnp.minimum(t, total - 1)
        i = lax.shift_right_logical(t, n_kv.bit_length() - 1)
        j = jnp.bitwise_and(t, n_kv - 1)
        q = q_ref[pl.ds(pl.multiple_of(i * tq, tq), tq), heads[hh]]
        k = k_ref[pl.ds(pl.multiple_of(j * tk, tk), tk), heads[hh]]
        st = lax.dot_general(k, q, (((1,), (1,)), ((), ())), preferred_element_type=F32)
        st_ref[slot, hh] = st
        return jnp.max(st, axis=0, keepdims=True)

    ones = jnp.ones((BF16_ROWS, MXU_DIM), BF16)

    def softmax_pv(j, hh, slot, m_old, acc, cmax):
        off = j * tk if isinstance(j, int) else pl.multiple_of(j * tk, tk)
        m_new = jnp.maximum(m_old, cmax)
        acc = jnp.exp2(m_old - m_new) * acc
        for t in range(tk // MXU_DIM):
            rows = slice(t * MXU_DIM, (t + 1) * MXU_DIM)
            pt = jnp.exp2(st_ref[slot, hh, rows, :] - m_new).astype(BF16)
            vt = vt_ref[hh * V_HEAD:(hh + 1) * V_HEAD, pl.ds(off + t * MXU_DIM, MXU_DIM)]
            acc = acc + _dot(jnp.concatenate([vt, ones], axis=0), pt)
        return m_new, acc

    def finish(i, carry):
        ot = jnp.concatenate([acc[:V_HEAD] / acc[V_HEAD:V_HEAD + 1] for _, acc, _ in carry], axis=0)
        row0 = i * tq if isinstance(i, int) else pl.multiple_of(i * tq, tq)
        o_ref[pl.ds(row0, tq), :] = ot.T.astype(o_ref.dtype)

    def body(bb, carry):
        jb = jnp.bitwise_and(bb, per_tile - 1)
        for step in range(unroll):
            slot = step % 2
            j = jb * unroll + step if tiles_per_body == 1 else step % n_kv
            new = []
            for hh in range(2):
                m_old, acc, cmax = carry[hh]
                cmax_next = scores(bb * unroll + step + 1, hh, 1 - slot)
                new.append(softmax_pv(j, hh, slot, m_old, acc, cmax) + (cmax_next,))
            carry = tuple(new)
            if tiles_per_body > 1 and (step + 1) % n_kv == 0:
                finish(bb * tiles_per_body + step // n_kv, carry)
                carry = tuple((m0, acc0, cmax) for _, _, cmax in carry)
        if tiles_per_body > 1:
            return carry
        i = lax.shift_right_logical(bb, per_tile.bit_length() - 1)
        if per_tile == 1:
            finish(i, carry)
            return tuple((m0, acc0, cmax) for _, _, cmax in carry)
        tile_done = jb == per_tile - 1
        pl.when(tile_done)(lambda: finish(i, carry))
        return tuple((jnp.where(tile_done, m0, m), jnp.where(tile_done, acc0, acc), cmax)
                     for m, acc, cmax in carry)

    m0 = jnp.full((1, tq), -jnp.inf, F32)
    acc0 = jnp.zeros((V_HEAD + BF16_ROWS, tq), F32)
    lax.fori_loop(0, total // unroll, body, tuple((m0, acc0, scores(0, hh, 0)) for hh in range(2)))


def _attn(q, k, vt, tq, tk):
    b, s, _ = q.shape
    resident = s * (4 * HEAD_PAD + 2 * V_HEAD) * 2
    mode = {} if 2 * resident <= VMEM_LIMIT // 4 else {"pipeline_mode": pl.Buffered(1)}
    return pl.pallas_call(
        functools.partial(_attn_kernel, tq=tq, tk=tk),
        scratch_shapes=[pltpu.VMEM((2, 2, tk, tq), F32)],
        grid=(b, N_HEADS // 2),
        in_specs=[pl.BlockSpec((None, s, 2 * HEAD_PAD), lambda bi, p: (bi, 0, p), **mode),
                  pl.BlockSpec((None, s, 2 * HEAD_PAD), lambda bi, p: (bi, 0, p), **mode),
                  pl.BlockSpec((None, 2 * V_HEAD, s), lambda bi, p: (bi, p, 0), **mode)],
        out_specs=pl.BlockSpec((None, s, 2 * V_HEAD), lambda bi, p: (bi, 0, p)),
        out_shape=jax.ShapeDtypeStruct((b, s, N_HEADS * V_HEAD), BF16),
        compiler_params=_params(2),
        name="attn",
    )(q, k, vt)


def _merge_kernel(x_ref, hf_ref, hb_ref, gg_ref, sa_ref, sb_ref, o_ref, mod_ref,
                  wr_ref, wm_ref, wo_ref, g_ref, b_ref, x1_ref, u2_ref):
    mod = mod_ref[...]
    ya = _dot((gg_ref[...] * (hf_ref[...] + hb_ref[...])).astype(BF16), wr_ref[...])
    yb = _dot(o_ref[...], wm_ref[...])
    mix = _dot((sa_ref[...] * ya + sb_ref[...] * yb).astype(BF16), wo_ref[...])
    x1 = _ln_plain(DN_ALPHA * x_ref[...] + mod[2:3] * mix) * g_ref[...] + b_ref[...]
    x1_ref[...] = x1
    u2_ref[...] = (_ln_plain(x1) * (1.0 + mod[4:5]) + mod[3:4]).astype(BF16)


def _merge(x, hf, hb, gg, sa, sb, o, mod, wr, wm, wo, g1, b1, tm):
    b, s, _ = x.shape
    tok = pl.BlockSpec((None, tm, D_MODEL), lambda bi, i: (bi, i, 0))
    return pl.pallas_call(
        _merge_kernel,
        grid=(b, s // tm),
        in_specs=[tok] * 7 + [pl.BlockSpec((None, 8, D_MODEL), lambda bi, i: (bi, 0, 0)),
                              _const_spec(wr.shape), _const_spec(wm.shape), _const_spec(wo.shape),
                              _const_spec(g1.shape), _const_spec(b1.shape)],
        out_specs=[tok, tok],
        out_shape=[jax.ShapeDtypeStruct((b, s, D_MODEL), F32),
                   jax.ShapeDtypeStruct((b, s, D_MODEL), BF16)],
        compiler_params=_params(2),
        name="merge",
    )(x, hf, hb, gg, sa, sb, o, mod, wr, wm, wo, g1, b1)


def _ffn_kernel(u_ref, up_ref, un_ref, x1_ref, mod_ref, wv_ref, wg_ref, cw_ref, cb_ref, wd_ref,
                g_ref, b_ref, y_ref, ue_ref, hv_ref, hg_ref, act_ref, *, tm):
    i = pl.program_id(1)
    last = pl.num_programs(1) - 1
    halo = BF16_ROWS
    keep_prev = (i > 0).astype(BF16)
    keep_next = (i < last).astype(BF16)
    ue_ref[0:halo, :] = up_ref[...] * keep_prev
    ue_ref[halo:halo + tm, :] = u_ref[...]
    ue_ref[halo + tm:, :] = un_ref[...] * keep_next
    ue = ue_ref[...]

    def up(j):
        sl = slice(j * FF_CHUNK, (j + 1) * FF_CHUNK)
        hv_ref[j % 2] = _dot(ue, wv_ref[:, sl])
        hg_ref[j % 2] = _dot(ue, wg_ref[:, sl])

    def conv(h_ref, slot, w, bias):
        return (bias + w[0:1] * h_ref[slot, pl.ds(halo - 1, tm), :]
                + w[1:2] * h_ref[slot, pl.ds(halo, tm), :]
                + w[2:3] * h_ref[slot, pl.ds(halo + 1, tm), :])

    up(0)
    for j in range(N_FF_CHUNKS):
        if j + 1 < N_FF_CHUNKS:
            up(j + 1)
        sl = slice(j * FF_CHUNK, (j + 1) * FF_CHUNK)
        slg = slice(D_FF + j * FF_CHUNK, D_FF + (j + 1) * FF_CHUNK)
        cv = conv(hv_ref, j % 2, cw_ref[:, sl], cb_ref[:, sl])
        cg = conv(hg_ref, j % 2, cw_ref[:, slg], cb_ref[:, slg])
        act_ref[:, sl] = (cg * jax.nn.sigmoid(cg) * cv).astype(BF16)
    y = _dot(act_ref[...], wd_ref[...])
    mod = mod_ref[...]
    y_ref[...] = _ln_plain(DN_ALPHA * x1_ref[...] + mod[5:6] * y) * g_ref[...] + b_ref[...]


def _ffn(u2, x1, mod, wv, wg, cw, cb, wd, g2, b2, tm):
    b, s, _ = x1.shape
    hpb = tm // BF16_ROWS
    nhb = s // BF16_ROWS
    tok = pl.BlockSpec((None, tm, D_MODEL), lambda bi, i: (bi, i, 0))
    return pl.pallas_call(
        functools.partial(_ffn_kernel, tm=tm),
        grid=(b, s // tm),
        in_specs=[tok,
                  pl.BlockSpec((None, BF16_ROWS, D_MODEL),
                               lambda bi, i: (bi, jnp.maximum(i * hpb - 1, 0), 0)),
                  pl.BlockSpec((None, BF16_ROWS, D_MODEL),
                               lambda bi, i: (bi, jnp.minimum((i + 1) * hpb, nhb - 1), 0)),
                  tok,
                  pl.BlockSpec((None, 8, D_MODEL), lambda bi, i: (bi, 0, 0)),
                  _const_spec(wv.shape), _const_spec(wg.shape), _const_spec(cw.shape),
                  _const_spec(cb.shape), _const_spec(wd.shape), _const_spec(g2.shape),
                  _const_spec(b2.shape)],
        out_specs=tok,
        out_shape=jax.ShapeDtypeStruct((b, s, D_MODEL), F32),
        scratch_shapes=[pltpu.VMEM((tm + 2 * BF16_ROWS, D_MODEL), BF16),
                        pltpu.VMEM((2, tm + 2 * BF16_ROWS, FF_CHUNK), F32),
                        pltpu.VMEM((2, tm + 2 * BF16_ROWS, FF_CHUNK), F32),
                        pltpu.VMEM((tm, D_FF), BF16)],
        compiler_params=_params(2),
        name="ffn",
    )(u2, u2, u2, x1, mod, wv, wg, cw, cb, wd, g2, b2)


def _rope_table(s):
    inv = 1.0 / (ROPE_BASE ** (jnp.arange(0, QK_ROPE, 2, dtype=F32) / QK_ROPE))
    ang = jnp.arange(s, dtype=F32)[:, None] * inv[None, :]
    cos, sin = jnp.cos(ang), jnp.sin(ang)
    one = jnp.ones((s, QK_NOPE), F32)
    zero = jnp.zeros((s, HALF_ROPE), F32)
    tail = jnp.zeros((s, HEAD_PAD - QK_NOPE - QK_ROPE), F32)
    zn = jnp.zeros((s, QK_NOPE), F32)
    cos_t = jnp.concatenate([one, cos, cos, tail], axis=1)
    sin_lo = jnp.concatenate([zn, -sin, zero, tail], axis=1)
    sin_hi = jnp.concatenate([zn, zero, sin, tail], axis=1)
    return jnp.stack([cos_t, sin_lo, sin_hi])


def _prep_weights(w_in, lru_w_r, lru_w_i, w_q_b, w_kv_b, w_up):
    o1, o2 = D_RNN, 2 * D_RNN
    o3, o4, o5 = o2 + Q_LORA, o2 + Q_LORA + KV_LORA, o2 + Q_LORA + KV_LORA + QK_ROPE
    o6 = o5 + D_MODEL
    kr = jnp.pad(w_in[:, o4:o5], ((0, 0), (QK_NOPE, HEAD_PAD - QK_NOPE - QK_ROPE)))
    w1 = jnp.concatenate([w_in[:, :o2], w_in[:, o5:o6], w_in[:, o6:], w_in[:, o2:o4], kr],
                         axis=1).astype(BF16)
    eye = jnp.eye(RNN_BLOCKS // N_GATE_GROUPS, dtype=F32)

    def blockdiag(w):
        wgp = w.reshape(2, N_GATE_GROUPS, RNN_BLOCKS // N_GATE_GROUPS, RNN_BLOCK, RNN_BLOCK)
        return jnp.einsum("dgnkj,nm->dgnkmj", wgp, eye).reshape(2, N_GATE_GROUPS, GATE_GROUP, GATE_GROUP)

    wgate = jnp.concatenate([blockdiag(lru_w_r), blockdiag(lru_w_i)], axis=-1).astype(BF16)
    qscale = (QK_NOPE + QK_ROPE) ** -0.5 * math.log2(math.e)
    wq = jnp.pad((w_q_b * qscale).reshape(Q_LORA, N_HEADS, QK_NOPE + QK_ROPE),
                 ((0, 0), (0, 0), (0, HEAD_PAD - QK_NOPE - QK_ROPE)))
    wq = wq.reshape(Q_LORA, N_HEADS * HEAD_PAD).astype(BF16)
    wkv = w_kv_b.reshape(KV_LORA, N_HEADS, QK_NOPE + V_HEAD)
    wk = jnp.pad(wkv[:, :, :QK_NOPE], ((0, 0), (0, 0), (0, HEAD_PAD - QK_NOPE)))
    wk = wk.reshape(KV_LORA, N_HEADS * HEAD_PAD).astype(BF16)
    wvt = wkv[:, :, QK_NOPE:].reshape(KV_LORA, N_HEADS * V_HEAD).T.astype(BF16)
    return w1, wgate, wq, wk, wvt, w_up[:, :D_FF].astype(BF16), w_up[:, D_FF:].astype(BF16)


def _tile(s, pref):
    return min(s, pref)


def _layer(x, mod, p, *, tm_in=256, ts=256, tq=256, tk=2048, tm_merge=256, tm_ffn=512):
    s = x.shape[1]
    rope_tab = _rope_table(s)
    xr, gg, sa, sb, q, k, vt = _inproj(x, mod, rope_tab, p["w1"], p["q_norm"], p["wq"], p["kv_norm"],
                                       p["wk"], p["wvt"], _tile(s, tm_in))
    hf, hb = _lru(xr, p["rnn_conv_w"], p["rnn_conv_b"], p["wgate"], p["lru_b_r"], p["lru_b_i"],
                  p["lru_lam"], _tile(s, ts))
    o = _attn(q, k, vt, _tile(s, tq), _tile(s // 2, tk))
    x1, u2 = _merge(x, hf, hb, gg, sa, sb, o, mod, p["w_rnn_proj"], p["w_mla_proj"], p["w_out"],
                    p["ln1_g"], p["ln1_b"], _tile(s, tm_merge))
    return _ffn(u2, x1, mod, p["w_up_v"], p["w_up_g"], p["ffn_conv_w"], p["ffn_conv_b"], p["w_down"],
                p["ln2_g"], p["ln2_b"], _tile(s, tm_ffn))


def kernel(x_prompt, x_sample, c_prompt, c_sample, w_ada, b_ada, w_in, rnn_conv_w, rnn_conv_b, lru_w_r, lru_b_r, lru_w_i, lru_b_i, lru_lam, w_rnn_proj, q_norm, w_q_b, kv_norm, w_kv_b, w_mla_proj, w_out, ln1_g, ln1_b, w_up, ffn_conv_w, ffn_conv_b, w_down, ln2_g, ln2_b):
    assert w_ada.shape[0] == DEPTH
    l = 0
    nb_p, nb_s = c_prompt.shape[0], c_sample.shape[0]
    rows = -(-(nb_p + nb_s) // SUBLANES) * SUBLANES
    c_all = jnp.concatenate([c_prompt, c_sample, jnp.zeros((rows - nb_p - nb_s, D_MODEL), F32)])
    mod = _ada(c_all, w_ada[l], b_ada[l][None, :]).reshape(rows, 6, D_MODEL)
    mod = jnp.pad(mod, ((0, 0), (0, 2), (0, 0)))
    w1, wgate, wq, wk, wvt, w_up_v, w_up_g = _prep_weights(w_in[l], lru_w_r[l], lru_w_i[l], w_q_b[l],
                                                         w_kv_b[l], w_up[l])
    row = lambda a: a[l][None, :]
    p = {
        "w1": w1, "wgate": wgate, "wq": wq, "wk": wk, "wvt": wvt, "w_up_v": w_up_v, "w_up_g": w_up_g,
        "q_norm": row(q_norm), "kv_norm": row(kv_norm),
        "rnn_conv_w": rnn_conv_w[l], "rnn_conv_b": row(rnn_conv_b),
        "lru_b_r": lru_b_r[l], "lru_b_i": lru_b_i[l], "lru_lam": lru_lam[l],
        "w_rnn_proj": w_rnn_proj[l].astype(BF16), "w_mla_proj": w_mla_proj[l].astype(BF16),
        "w_out": w_out[l].astype(BF16), "ln1_g": row(ln1_g), "ln1_b": row(ln1_b),
        "ffn_conv_w": ffn_conv_w[l], "ffn_conv_b": row(ffn_conv_b),
        "w_down": w_down[l].astype(BF16), "ln2_g": row(ln2_g), "ln2_b": row(ln2_b),
    }
    y_prompt = _layer(x_prompt, mod[:nb_p], p)
    y_sample = _layer(x_sample, mod[nb_p:nb_p + nb_s], p)
    return (y_prompt, y_sample)
```

```python
import functools
import math

import jax
import jax.numpy as jnp
from jax import lax
from jax.experimental import pallas as pl
from jax.experimental.pallas import tpu as pltpu

F32 = jnp.float32
BF16 = jnp.bfloat16

D_MODEL = 1024
D_RNN = 1024
RNN_BLOCKS = 16
RNN_BLOCK = D_RNN // RNN_BLOCKS
RNN_CONV = 4
RNN_CONV_LEFT = 2
LRU_C = 8.0
N_HEADS = 16
QK_NOPE = 64
QK_ROPE = 32
V_HEAD = 64
Q_LORA = 384
KV_LORA = 256
ROPE_BASE = 10000.0
D_FF = 2816
FFN_CONV = 3
LN_EPS = 1e-5
RMS_EPS = 1e-6
DEPTH = 1
DN_ALPHA = (2.0 * DEPTH) ** 0.25

LANES = 128
SUBLANES = 8
BF16_ROWS = 16
MXU_DIM = 256
VMEM_LIMIT = 56 * 1024 * 1024

HEAD_PAD = LANES
HALF_ROPE = QK_ROPE // 2
GATE_GROUP = MXU_DIM
N_GATE_GROUPS = D_RNN // GATE_GROUP
ROW_SUBTILE = MXU_DIM
ATTN_UNROLL = 8
FF_CHUNK = MXU_DIM
N_FF_CHUNKS = D_FF // FF_CHUNK
OFF_QL = 4 * D_MODEL
OFF_KVL = OFF_QL + Q_LORA
OFF_KR = OFF_KVL + KV_LORA
N_IN_PAD = OFF_KR + HEAD_PAD


def _const_spec(shape):
    nd = len(shape)
    return pl.BlockSpec(shape, lambda *_: (0,) * nd, pipeline_mode=pl.Buffered(1))


def _params(n_grid):
    return pltpu.CompilerParams(dimension_semantics=("arbitrary",) * n_grid,
                                vmem_limit_bytes=VMEM_LIMIT)


def _ln_plain(x):
    mu = jnp.mean(x, axis=-1, keepdims=True)
    xc = x - mu
    var = jnp.mean(xc * xc, axis=-1, keepdims=True)
    return xc * lax.rsqrt(var + LN_EPS)


def _rms(x, g):
    return x * lax.rsqrt(jnp.mean(x * x, axis=-1, keepdims=True) + RMS_EPS) * g


def _dot(a, b):
    return jnp.dot(a, b, preferred_element_type=F32)


def _ada_kernel(c_ref, w_ref, b_ref, o_ref):
    c = c_ref[...]
    s = (c * jax.nn.sigmoid(c)).astype(BF16)
    o_ref[...] = _dot(s, w_ref[...].astype(BF16)) + b_ref[...]


def _ada(c_all, w_ada, b_ada, tn=512):
    rows, n = c_all.shape[0], w_ada.shape[1]
    return pl.pallas_call(
        _ada_kernel,
        grid=(n // tn,),
        in_specs=[pl.BlockSpec((rows, D_MODEL), lambda j: (0, 0)),
                  pl.BlockSpec((D_MODEL, tn), lambda j: (0, j)),
                  pl.BlockSpec((1, tn), lambda j: (0, j))],
        out_specs=pl.BlockSpec((rows, tn), lambda j: (0, j)),
        out_shape=jax.ShapeDtypeStruct((rows, n), F32),
        compiler_params=_params(1),
        name="ada",
    )(c_all, w_ada, b_ada)


def _inproj_kernel(x_ref, mod_ref, rope_ref, w1_ref, qn_ref, wq_ref, kvn_ref, wk_ref, wvt_ref,
                   xr_ref, gg_ref, sa_ref, sb_ref, q_ref, k_ref, vt_ref):
    mod = mod_ref[...]
    for r0 in range(0, x_ref.shape[0], ROW_SUBTILE):
        rows = slice(r0, r0 + ROW_SUBTILE)
        ub = (_ln_plain(x_ref[rows, :]) * (1.0 + mod[1:2]) + mod[0:1]).astype(BF16)

        def proj(lo, hi):
            return _dot(ub, w1_ref[:, lo:hi])

        cos_t, sin_lo, sin_hi = rope_ref[0, rows, :], rope_ref[1, rows, :], rope_ref[2, rows, :]

        def rope(t):
            return (t * cos_t + pltpu.roll(t, HEAD_PAD - HALF_ROPE, 1) * sin_lo
                    + pltpu.roll(t, HALF_ROPE, 1) * sin_hi)

        ql = proj(OFF_QL, OFF_KVL)
        kvl = proj(OFF_KVL, OFF_KR)
        kr = proj(OFF_KR, N_IN_PAD)
        xr_ref[rows, :] = proj(0, D_MODEL)
        qn = _rms(ql, qn_ref[...]).astype(BF16)
        kvn = _rms(kvl, kvn_ref[...]).astype(BF16)
        kpe = rope(kr)
        gg_ref[rows, :] = jax.nn.gelu(proj(D_MODEL, 2 * D_MODEL)).astype(BF16)
        qf = _dot(qn, wq_ref[...])
        kf = _dot(kvn, wk_ref[...])
        vt_ref[:, rows] = lax.dot_general(wvt_ref[...], kvn, (((1,), (1,)), ((), ())),
                                          preferred_element_type=F32).astype(BF16)
        sa_ref[rows, :] = jax.nn.sigmoid(proj(2 * D_MODEL, 3 * D_MODEL)).astype(BF16)
        for h in range(N_HEADS):
            sl = slice(h * HEAD_PAD, (h + 1) * HEAD_PAD)
            q_ref[rows, sl] = rope(qf[:, sl]).astype(BF16)
            k_ref[rows, sl] = (kf[:, sl] + kpe).astype(BF16)
        sb_ref[rows, :] = jax.nn.sigmoid(proj(3 * D_MODEL, 4 * D_MODEL)).astype(BF16)


def _inproj(x, mod, rope_tab, w1, qn, wq, kvn, wk, wv, tm):
    b, s, _ = x.shape
    tok = lambda w: pl.BlockSpec((None, tm, w), lambda bi, i: (bi, i, 0))
    shp = lambda w, dt: jax.ShapeDtypeStruct((b, s, w), dt)
    return pl.pallas_call(
        _inproj_kernel,
        grid=(b, s // tm),
        in_specs=[tok(D_MODEL),
                  pl.BlockSpec((None, 8, D_MODEL), lambda bi, i: (bi, 0, 0)),
                  pl.BlockSpec((3, tm, HEAD_PAD), lambda bi, i: (0, i, 0)),
                  _const_spec(w1.shape), _const_spec(qn.shape), _const_spec(wq.shape),
                  _const_spec(kvn.shape), _const_spec(wk.shape), _const_spec(wv.shape)],
        out_specs=[tok(D_RNN), tok(D_RNN), tok(D_MODEL), tok(D_MODEL),
                   tok(N_HEADS * HEAD_PAD), tok(N_HEADS * HEAD_PAD),
                   pl.BlockSpec((None, N_HEADS * V_HEAD, tm), lambda bi, i: (bi, 0, i))],
        out_shape=[shp(D_RNN, F32), shp(D_RNN, BF16), shp(D_MODEL, BF16), shp(D_MODEL, BF16),
                   shp(N_HEADS * HEAD_PAD, BF16), shp(N_HEADS * HEAD_PAD, BF16),
                   jax.ShapeDtypeStruct((b, N_HEADS * V_HEAD, s), BF16)],
        compiler_params=_params(2),
        name="inproj",
    )(x, mod, rope_tab, w1, qn, wq, kvn, wk, wv)


def _lru_kernel(xf_ref, xfp_ref, xfn_ref, xb_ref, xbp_ref, xbn_ref,
                cw_ref, cb_ref, wg_ref, br_ref, bi_ref, lam_ref,
                hf_ref, hb_ref, xs_ref, a_ref, u_ref, carry_ref, *, ts):
    c = pl.program_id(1)
    last = pl.num_programs(1) - 1

    @pl.when(c == 0)
    def _():
        carry_ref[...] = jnp.zeros_like(carry_ref)

    lam = lam_ref[...]
    sp = jnp.maximum(-lam, 0.0) + jnp.log1p(jnp.exp(-jnp.abs(lam)))
    cw = cw_ref[...]
    row = lax.broadcasted_iota(jnp.int32, (SUBLANES, D_RNN), 0)
    n_grp = ts // SUBLANES

    def gates(d, cur_ref, prev_ref, next_ref, keep_prev, keep_next):
        xs_ref[0:SUBLANES, :] = prev_ref[...] * keep_prev
        xs_ref[SUBLANES:SUBLANES + ts, :] = cur_ref[...]
        xs_ref[SUBLANES + ts:, :] = next_ref[...] * keep_next
        xc = cb_ref[...] + cw[0:1] * xs_ref[pl.ds(SUBLANES - RNN_CONV_LEFT, ts), :]
        for k in range(1, RNN_CONV):
            xc = xc + cw[k:k + 1] * xs_ref[pl.ds(SUBLANES - RNN_CONV_LEFT + k, ts), :]
        xcb = xc.astype(BF16)
        for g in range(N_GATE_GROUPS):
            sl = slice(g * GATE_GROUP, (g + 1) * GATE_GROUP)
            pre = _dot(xcb[:, sl], wg_ref[d, g])
            r = jax.nn.sigmoid(pre[:, :GATE_GROUP] + br_ref[d:d + 1, sl])
            i = jax.nn.sigmoid(pre[:, GATE_GROUP:] + bi_ref[d:d + 1, sl])
            log_a = (-LRU_C) * r * sp[d:d + 1, sl]
            a = jnp.exp(log_a)
            a_ref[d, :, sl] = a
            u_ref[d, :, sl] = jnp.sqrt(-jnp.tanh(log_a) * (a * a + 1.0)) * i * xc[:, sl]

    def scan(d, out_ref):
        def body(j, carry):
            g = j if d == 0 else n_grp - 1 - j
            off = pl.multiple_of(g * SUBLANES, SUBLANES)
            a8 = a_ref[d, pl.ds(off, SUBLANES), :]
            u8 = u_ref[d, pl.ds(off, SUBLANES), :]
            for sh in (1, 2, 4):
                if d == 0:
                    edge = row < sh
                    shift = sh
                else:
                    edge = row >= SUBLANES - sh
                    shift = SUBLANES - sh
                a_s = jnp.where(edge, 1.0, pltpu.roll(a8, shift, 0))
                u_s = jnp.where(edge, 0.0, pltpu.roll(u8, shift, 0))
                u8 = a8 * u_s + u8
                a8 = a8 * a_s
            h8 = u8 + a8 * carry
            out_ref[pl.ds(off, SUBLANES), :] = h8
            return h8[SUBLANES - 1:SUBLANES, :] if d == 0 else h8[0:1, :]

        carry_ref[d:d + 1, :] = lax.fori_loop(0, n_grp, body, carry_ref[d:d + 1, :])

    not_first = (c > 0).astype(F32)
    not_last = (c < last).astype(F32)
    gates(0, xf_ref, xfp_ref, xfn_ref, not_first, not_last)
    gates(1, xb_ref, xbp_ref, xbn_ref, not_last, not_first)
    scan(0, hf_ref)
    scan(1, hb_ref)


def _lru(xr, cw, cb, wg, br, bi, lam, ts):
    b, s, _ = xr.shape
    nc = s // ts
    hpb = ts // SUBLANES
    nhb = s // SUBLANES
    cur = lambda f: pl.BlockSpec((None, ts, D_RNN), lambda bi_, c: (bi_, f(c), 0))
    halo = lambda f: pl.BlockSpec((None, SUBLANES, D_RNN), lambda bi_, c: (bi_, f(c), 0))
    fwd = lambda c: c
    bwd = lambda c: nc - 1 - c
    prev = lambda f: (lambda c: jnp.maximum(f(c) * hpb - 1, 0))
    nxt = lambda f: (lambda c: jnp.minimum((f(c) + 1) * hpb, nhb - 1))
    return pl.pallas_call(
        functools.partial(_lru_kernel, ts=ts),
        grid=(b, nc),
        in_specs=[cur(fwd), halo(prev(fwd)), halo(nxt(fwd)),
                  cur(bwd), halo(prev(bwd)), halo(nxt(bwd)),
                  _const_spec(cw.shape), _const_spec(cb.shape), _const_spec(wg.shape),
                  _const_spec(br.shape), _const_spec(bi.shape), _const_spec(lam.shape)],
        out_specs=[cur(fwd), cur(bwd)],
        out_shape=[jax.ShapeDtypeStruct((b, s, D_RNN), F32)] * 2,
        scratch_shapes=[pltpu.VMEM((ts + 2 * SUBLANES, D_RNN), F32),
                        pltpu.VMEM((2, ts, D_RNN), F32),
                        pltpu.VMEM((2, ts, D_RNN), F32),
                        pltpu.VMEM((SUBLANES, D_RNN), F32)],
        compiler_params=_params(2),
        name="lru",
    )(xr, xr, xr, xr, xr, xr, cw, cb, wg, br, bi, lam)


def _attn_kernel(q_ref, k_ref, vt_ref, o_ref, st_ref, *, tq, tk):
    s_len = k_ref.shape[0]
    n_kv, n_q = s_len // tk, s_len // tq
    total = n_q * n_kv
    unroll = min(ATTN_UNROLL, total)
    assert unroll % 2 == 0 and total % unroll == 0 and n_kv & (n_kv - 1) == 0
    per_tile = max(n_kv // unroll, 1)
    tiles_per_body = max(unroll // n_kv, 1)
    assert per_tile & (per_tile - 1) == 0
    heads = [slice(hh * HEAD_PAD, (hh + 1) * HEAD_PAD) for hh in range(2)]

    def scores(t, hh, slot):
        t = jnp.minimum(t, total - 1)
        i = lax.shift_right_logical(t, n_kv.bit_length() - 1)
        j = jnp.bitwise_and(t, n_kv - 1)
        q = q_ref[pl.ds(pl.multiple_of(i * tq, tq), tq), heads[hh]]
        k = k_ref[pl.ds(pl.multiple_of(j * tk, tk), tk), heads[hh]]
        st = lax.dot_general(k, q, (((1,), (1,)), ((), ())), preferred_element_type=F32)
        st_ref[slot, hh] = st
        return jnp.max(st, axis=0, keepdims=True)

    ones = jnp.ones((BF16_ROWS, MXU_DIM), BF16)

    def softmax_pv(j, hh, slot, m_old, acc, cmax):
        off = j * tk if isinstance(j, int) else pl.multiple_of(j * tk, tk)
        m_new = jnp.maximum(m_old, cmax)
        acc = jnp.exp2(m_old - m_new) * acc
        for t in range(tk // MXU_DIM):
            rows = slice(t * MXU_DIM, (t + 1) * MXU_DIM)
            pt = jnp.exp2(st_ref[slot, hh, rows, :] - m_new).astype(BF16)
            vt = vt_ref[hh * V_HEAD:(hh + 1) * V_HEAD, pl.ds(off + t * MXU_DIM, MXU_DIM)]
            acc = acc + _dot(jnp.concatenate([vt, ones], axis=0), pt)
        return m_new, acc

    def finish(i, carry):
        ot = jnp.concatenate([acc[:V_HEAD] / acc[V_HEAD:V_HEAD + 1] for _, acc, _ in carry], axis=0)
        row0 = i * tq if isinstance(i, int) else pl.multiple_of(i * tq, tq)
        o_ref[pl.ds(row0, tq), :] = ot.T.astype(o_ref.dtype)

    def body(bb, carry):
        jb = jnp.bitwise_and(bb, per_tile - 1)
        for step in range(unroll):
            slot = step % 2
            j = jb * unroll + step if tiles_per_body == 1 else step % n_kv
            new = []
            for hh in range(2):
                m_old, acc, cmax = carry[hh]
                cmax_next = scores(bb * unroll + step + 1, hh, 1 - slot)
                new.append(softmax_pv(j, hh, slot, m_old, acc, cmax) + (cmax_next,))
            carry = tuple(new)
            if tiles_per_body > 1 and (step + 1) % n_kv == 0:
                finish(bb * tiles_per_body + step // n_kv, carry)
                carry = tuple((m0, acc0, cmax) for _, _, cmax in carry)
        if tiles_per_body > 1:
            return carry
        i = lax.shift_right_logical(bb, per_tile.bit_length() - 1)
        if per_tile == 1:
            finish(i, carry)
            return tuple((m0, acc0, cmax) for _, _, cmax in carry)
        tile_done = jb == per_tile - 1
        pl.when(tile_done)(lambda: finish(i, carry))
        return tuple((jnp.where(tile_done, m0, m), jnp.where(tile_done, acc0, acc), cmax)
                     for m, acc, cmax in carry)

    m0 = jnp.full((1, tq), -jnp.inf, F32)
    acc0 = jnp.zeros((V_HEAD + BF16_ROWS, tq), F32)
    lax.fori_loop(0, total // unroll, body, tuple((m0, acc0, scores(0, hh, 0)) for hh in range(2)))


def _attn(q, k, vt, tq, tk):
    b, s, _ = q.shape
    resident = s * (4 * HEAD_PAD + 2 * V_HEAD) * 2
    mode = {} if 2 * resident <= VMEM_LIMIT // 4 else {"pipeline_mode": pl.Buffered(1)}
    return pl.pallas_call(
        functools.partial(_attn_kernel, tq=tq, tk=tk),
        scratch_shapes=[pltpu.VMEM((2, 2, tk, tq), F32)],
        grid=(b, N_HEADS // 2),
        in_specs=[pl.BlockSpec((None, s, 2 * HEAD_PAD), lambda bi, p: (bi, 0, p), **mode),
                  pl.BlockSpec((None, s, 2 * HEAD_PAD), lambda bi, p: (bi, 0, p), **mode),
                  pl.BlockSpec((None, 2 * V_HEAD, s), lambda bi, p: (bi, p, 0), **mode)],
        out_specs=pl.BlockSpec((None, s, 2 * V_HEAD), lambda bi, p: (bi, 0, p)),
        out_shape=jax.ShapeDtypeStruct((b, s, N_HEADS * V_HEAD), BF16),
        compiler_params=_params(2),
        name="attn",
    )(q, k, vt)


def _merge_kernel(x_ref, hf_ref, hb_ref, gg_ref, sa_ref, sb_ref, o_ref, mod_ref,
                  wr_ref, wm_ref, wo_ref, g_ref, b_ref, x1_ref, u2_ref):
    mod = mod_ref[...]
    for r0 in range(0, x_ref.shape[0], ROW_SUBTILE):
        rows = slice(r0, r0 + ROW_SUBTILE)
        yb = _dot(o_ref[rows, :], wm_ref[...])
        ya = _dot((gg_ref[rows, :] * (hf_ref[rows, :] + hb_ref[rows, :])).astype(BF16), wr_ref[...])
        mix = _dot((sa_ref[rows, :] * ya + sb_ref[rows, :] * yb).astype(BF16), wo_ref[...])
        x1 = _ln_plain(DN_ALPHA * x_ref[rows, :] + mod[2:3] * mix) * g_ref[...] + b_ref[...]
        x1_ref[rows, :] = x1
        u2_ref[rows, :] = (_ln_plain(x1) * (1.0 + mod[4:5]) + mod[3:4]).astype(BF16)


def _merge(x, hf, hb, gg, sa, sb, o, mod, wr, wm, wo, g1, b1, tm):
    b, s, _ = x.shape
    tok = pl.BlockSpec((None, tm, D_MODEL), lambda bi, i: (bi, i, 0))
    return pl.pallas_call(
        _merge_kernel,
        grid=(b, s // tm),
        in_specs=[tok] * 7 + [pl.BlockSpec((None, 8, D_MODEL), lambda bi, i: (bi, 0, 0)),
                              _const_spec(wr.shape), _const_spec(wm.shape), _const_spec(wo.shape),
                              _const_spec(g1.shape), _const_spec(b1.shape)],
        out_specs=[tok, tok],
        out_shape=[jax.ShapeDtypeStruct((b, s, D_MODEL), F32),
                   jax.ShapeDtypeStruct((b, s, D_MODEL), BF16)],
        compiler_params=_params(2),
        name="merge",
    )(x, hf, hb, gg, sa, sb, o, mod, wr, wm, wo, g1, b1)


def _ffn_kernel(u_ref, up_ref, un_ref, x1_ref, mod_ref, wv_ref, wg_ref, cw_ref, cb_ref, wd_ref,
                g_ref, b_ref, y_ref, ue_ref, hv_ref, hg_ref, act_ref, *, tm):
    i = pl.program_id(1)
    last = pl.num_programs(1) - 1
    halo = BF16_ROWS
    keep_prev = (i > 0).astype(BF16)
    keep_next = (i < last).astype(BF16)
    ue_ref[0:halo, :] = up_ref[...] * keep_prev
    ue_ref[halo:halo + tm, :] = u_ref[...]
    ue_ref[halo + tm:, :] = un_ref[...] * keep_next
    ue = ue_ref[...]

    def up(j):
        sl = slice(j * FF_CHUNK, (j + 1) * FF_CHUNK)
        hv_ref[j % 2] = _dot(ue, wv_ref[:, sl])
        hg_ref[j % 2] = _dot(ue, wg_ref[:, sl])

    def conv(h_ref, slot, w, bias):
        return (bias + w[0:1] * h_ref[slot, pl.ds(halo - 1, tm), :]
                + w[1:2] * h_ref[slot, pl.ds(halo, tm), :]
                + w[2:3] * h_ref[slot, pl.ds(halo + 1, tm), :])

    up(0)
    for j in range(N_FF_CHUNKS):
        if j + 1 < N_FF_CHUNKS:
            up(j + 1)
        sl = slice(j * FF_CHUNK, (j + 1) * FF_CHUNK)
        slg = slice(D_FF + j * FF_CHUNK, D_FF + (j + 1) * FF_CHUNK)
        cv = conv(hv_ref, j % 2, cw_ref[:, sl], cb_ref[:, sl])
        cg = conv(hg_ref, j % 2, cw_ref[:, slg], cb_ref[:, slg])
        act_ref[:, sl] = (cg * jax.nn.sigmoid(cg) * cv).astype(BF16)
    y = _dot(act_ref[...], wd_ref[...])
    mod = mod_ref[...]
    y_ref[...] = _ln_plain(DN_ALPHA * x1_ref[...] + mod[5:6] * y) * g_ref[...] + b_ref[...]


def _ffn(u2, x1, mod, wv, wg, cw, cb, wd, g2, b2, tm):
    b, s, _ = x1.shape
    hpb = tm // BF16_ROWS
    nhb = s // BF16_ROWS
    tok = pl.BlockSpec((None, tm, D_MODEL), lambda bi, i: (bi, i, 0))
    return pl.pallas_call(
        functools.partial(_ffn_kernel, tm=tm),
        grid=(b, s // tm),
        in_specs=[tok,
                  pl.BlockSpec((None, BF16_ROWS, D_MODEL),
                               lambda bi, i: (bi, jnp.maximum(i * hpb - 1, 0), 0)),
                  pl.BlockSpec((None, BF16_ROWS, D_MODEL),
                               lambda bi, i: (bi, jnp.minimum((i + 1) * hpb, nhb - 1), 0)),
                  tok,
                  pl.BlockSpec((None, 8, D_MODEL), lambda bi, i: (bi, 0, 0)),
                  _const_spec(wv.shape), _const_spec(wg.shape), _const_spec(cw.shape),
                  _const_spec(cb.shape), _const_spec(wd.shape), _const_spec(g2.shape),
                  _const_spec(b2.shape)],
        out_specs=tok,
        out_shape=jax.ShapeDtypeStruct((b, s, D_MODEL), F32),
        scratch_shapes=[pltpu.VMEM((tm + 2 * BF16_ROWS, D_MODEL), BF16),
                        pltpu.VMEM((2, tm + 2 * BF16_ROWS, FF_CHUNK), F32),
                        pltpu.VMEM((2, tm + 2 * BF16_ROWS, FF_CHUNK), F32),
                        pltpu.VMEM((tm, D_FF), BF16)],
        compiler_params=_params(2),
        name="ffn",
    )(u2, u2, u2, x1, mod, wv, wg, cw, cb, wd, g2, b2)


def _rope_table(s):
    inv = 1.0 / (ROPE_BASE ** (jnp.arange(0, QK_ROPE, 2, dtype=F32) / QK_ROPE))
    ang = jnp.arange(s, dtype=F32)[:, None] * inv[None, :]
    cos, sin = jnp.cos(ang), jnp.sin(ang)
    one = jnp.ones((s, QK_NOPE), F32)
    zero = jnp.zeros((s, HALF_ROPE), F32)
    tail = jnp.zeros((s, HEAD_PAD - QK_NOPE - QK_ROPE), F32)
    zn = jnp.zeros((s, QK_NOPE), F32)
    cos_t = jnp.concatenate([one, cos, cos, tail], axis=1)
    sin_lo = jnp.concatenate([zn, -sin, zero, tail], axis=1)
    sin_hi = jnp.concatenate([zn, zero, sin, tail], axis=1)
    return jnp.stack([cos_t, sin_lo, sin_hi])


def _prep_weights(w_in, lru_w_r, lru_w_i, w_q_b, w_kv_b, w_up):
    o1, o2 = D_RNN, 2 * D_RNN
    o3, o4, o5 = o2 + Q_LORA, o2 + Q_LORA + KV_LORA, o2 + Q_LORA + KV_LORA + QK_ROPE
    o6 = o5 + D_MODEL
    kr = jnp.pad(w_in[:, o4:o5], ((0, 0), (QK_NOPE, HEAD_PAD - QK_NOPE - QK_ROPE)))
    w1 = jnp.concatenate([w_in[:, :o2], w_in[:, o5:o6], w_in[:, o6:], w_in[:, o2:o4], kr],
                         axis=1).astype(BF16)
    eye = jnp.eye(RNN_BLOCKS // N_GATE_GROUPS, dtype=F32)

    def blockdiag(w):
        wgp = w.reshape(2, N_GATE_GROUPS, RNN_BLOCKS // N_GATE_GROUPS, RNN_BLOCK, RNN_BLOCK)
        return jnp.einsum("dgnkj,nm->dgnkmj", wgp, eye).reshape(2, N_GATE_GROUPS, GATE_GROUP, GATE_GROUP)

    wgate = jnp.concatenate([blockdiag(lru_w_r), blockdiag(lru_w_i)], axis=-1).astype(BF16)
    qscale = (QK_NOPE + QK_ROPE) ** -0.5 * math.log2(math.e)
    wq = jnp.pad((w_q_b * qscale).reshape(Q_LORA, N_HEADS, QK_NOPE + QK_ROPE),
                 ((0, 0), (0, 0), (0, HEAD_PAD - QK_NOPE - QK_ROPE)))
    wq = wq.reshape(Q_LORA, N_HEADS * HEAD_PAD).astype(BF16)
    wkv = w_kv_b.reshape(KV_LORA, N_HEADS, QK_NOPE + V_HEAD)
    wk = jnp.pad(wkv[:, :, :QK_NOPE], ((0, 0), (0, 0), (0, HEAD_PAD - QK_NOPE)))
    wk = wk.reshape(KV_LORA, N_HEADS * HEAD_PAD).astype(BF16)
    wvt = wkv[:, :, QK_NOPE:].reshape(KV_LORA, N_HEADS * V_HEAD).T.astype(BF16)
    return w1, wgate, wq, wk, wvt, w_up[:, :D_FF].astype(BF16), w_up[:, D_FF:].astype(BF16)


def _tile(s, pref):
    return min(s, pref)


def _layer(x, mod, p, *, tm_in=512, ts=256, tq=256, tk=2048, tm_merge=512, tm_ffn=512):
    s = x.shape[1]
    rope_tab = _rope_table(s)
    xr, gg, sa, sb, q, k, vt = _inproj(x, mod, rope_tab, p["w1"], p["q_norm"], p["wq"], p["kv_norm"],
                                       p["wk"], p["wvt"], _tile(s, tm_in))
    hf, hb = _lru(xr, p["rnn_conv_w"], p["rnn_conv_b"], p["wgate"], p["lru_b_r"], p["lru_b_i"],
                  p["lru_lam"], _tile(s, ts))
    o = _attn(q, k, vt, _tile(s, tq), _tile(s // 2, tk))
    x1, u2 = _merge(x, hf, hb, gg, sa, sb, o, mod, p["w_rnn_proj"], p["w_mla_proj"], p["w_out"],
                    p["ln1_g"], p["ln1_b"], _tile(s, tm_merge))
    return _ffn(u2, x1, mod, p["w_up_v"], p["w_up_g"], p["ffn_conv_w"], p["ffn_conv_b"], p["w_down"],
                p["ln2_g"], p["ln2_b"], _tile(s, tm_ffn))


def kernel(x_prompt, x_sample, c_prompt, c_sample, w_ada, b_ada, w_in, rnn_conv_w, rnn_conv_b, lru_w_r, lru_b_r, lru_w_i, lru_b_i, lru_lam, w_rnn_proj, q_norm, w_q_b, kv_norm, w_kv_b, w_mla_proj, w_out, ln1_g, ln1_b, w_up, ffn_conv_w, ffn_conv_b, w_down, ln2_g, ln2_b):
    assert w_ada.shape[0] == DEPTH
    l = 0
    nb_p, nb_s = c_prompt.shape[0], c_sample.shape[0]
    rows = -(-(nb_p + nb_s) // SUBLANES) * SUBLANES
    c_all = jnp.concatenate([c_prompt, c_sample, jnp.zeros((rows - nb_p - nb_s, D_MODEL), F32)])
    mod = _ada(c_all, w_ada[l], b_ada[l][None, :]).reshape(rows, 6, D_MODEL)
    mod = jnp.pad(mod, ((0, 0), (0, 2), (0, 0)))
    w1, wgate, wq, wk, wvt, w_up_v, w_up_g = _prep_weights(w_in[l], lru_w_r[l], lru_w_i[l], w_q_b[l],
                                                         w_kv_b[l], w_up[l])
    row = lambda a: a[l][None, :]
    p = {
        "w1": w1, "wgate": wgate, "wq": wq, "wk": wk, "wvt": wvt, "w_up_v": w_up_v, "w_up_g": w_up_g,
        "q_norm": row(q_norm), "kv_norm": row(kv_norm),
        "rnn_conv_w": rnn_conv_w[l], "rnn_conv_b": row(rnn_conv_b),
        "lru_b_r": lru_b_r[l], "lru_b_i": lru_b_i[l], "lru_lam": lru_lam[l],
        "w_rnn_proj": w_rnn_proj[l].astype(BF16), "w_mla_proj": w_mla_proj[l].astype(BF16),
        "w_out": w_out[l].astype(BF16), "ln1_g": row(ln1_g), "ln1_b": row(ln1_b),
        "ffn_conv_w": ffn_conv_w[l], "ffn_conv_b": row(ffn_conv_b),
        "w_down": w_down[l].astype(BF16), "ln2_g": row(ln2_g), "ln2_b": row(ln2_b),
    }
    y_prompt = _layer(x_prompt, mod[:nb_p], p)
    y_sample = _layer(x_sample, mod[nb_p:nb_p + nb_s], p)
    return (y_prompt, y_sample)
```

```python
import functools
import math

import jax
import jax.numpy as jnp
from jax import lax
from jax.experimental import pallas as pl
from jax.experimental.pallas import tpu as pltpu

F32 = jnp.float32
BF16 = jnp.bfloat16

D_MODEL = 1024
D_RNN = 1024
RNN_BLOCKS = 16
RNN_BLOCK = D_RNN // RNN_BLOCKS
RNN_CONV = 4
RNN_CONV_LEFT = 2
LRU_C = 8.0
N_HEADS = 16
QK_NOPE = 64
QK_ROPE = 32
V_HEAD = 64
Q_LORA = 384
KV_LORA = 256
ROPE_BASE = 10000.0
D_FF = 2816
FFN_CONV = 3
LN_EPS = 1e-5
RMS_EPS = 1e-6
DEPTH = 1
DN_ALPHA = (2.0 * DEPTH) ** 0.25

LANES = 128
SUBLANES = 8
BF16_ROWS = 16
MXU_DIM = 256
VMEM_LIMIT = 56 * 1024 * 1024

HEAD_PAD = LANES
HALF_ROPE = QK_ROPE // 2
GATE_GROUP = MXU_DIM
N_GATE_GROUPS = D_RNN // GATE_GROUP
ROW_SUBTILE = MXU_DIM
ATTN_UNROLL = 8
FF_CHUNK = MXU_DIM
N_FF_CHUNKS = D_FF // FF_CHUNK
OFF_QL = 4 * D_MODEL
OFF_KVL = OFF_QL + Q_LORA
OFF_KR = OFF_KVL + KV_LORA
N_IN_PAD = OFF_KR + HEAD_PAD


def _const_spec(shape):
    nd = len(shape)
    return pl.BlockSpec(shape, lambda *_: (0,) * nd, pipeline_mode=pl.Buffered(1))


def _params(n_grid):
    return pltpu.CompilerParams(dimension_semantics=("arbitrary",) * n_grid,
                                vmem_limit_bytes=VMEM_LIMIT)


def _ln_plain(x):
    mu = jnp.mean(x, axis=-1, keepdims=True)
    xc = x - mu
    var = jnp.mean(xc * xc, axis=-1, keepdims=True)
    return xc * lax.rsqrt(var + LN_EPS)


def _rms(x, g):
    return x * lax.rsqrt(jnp.mean(x * x, axis=-1, keepdims=True) + RMS_EPS) * g


def _dot(a, b):
    return jnp.dot(a, b, preferred_element_type=F32)


def _ada_kernel(c_ref, w_ref, b_ref, o_ref):
    c = c_ref[...]
    s = (c * jax.nn.sigmoid(c)).astype(BF16)
    o_ref[...] = _dot(s, w_ref[...].astype(BF16)) + b_ref[...]


def _ada(c_all, w_ada, b_ada, tn=512):
    rows, n = c_all.shape[0], w_ada.shape[1]
    return pl.pallas_call(
        _ada_kernel,
        grid=(n // tn,),
        in_specs=[pl.BlockSpec((rows, D_MODEL), lambda j: (0, 0)),
                  pl.BlockSpec((D_MODEL, tn), lambda j: (0, j)),
                  pl.BlockSpec((1, tn), lambda j: (0, j))],
        out_specs=pl.BlockSpec((rows, tn), lambda j: (0, j)),
        out_shape=jax.ShapeDtypeStruct((rows, n), F32),
        compiler_params=_params(1),
        name="ada",
    )(c_all, w_ada, b_ada)


def _inproj_kernel(x_ref, mod_ref, rope_ref, w1a_ref, w1g_ref, w1l_ref, qn_ref, wq_ref, kvn_ref,
                   wk_ref, wvt_ref, xr_ref, gg_ref, sa_ref, sb_ref, q_ref, k_ref, vt_ref):
    segments = ((0, w1a_ref), (2 * D_MODEL, w1g_ref), (OFF_QL, w1l_ref))
    mod = mod_ref[...]
    for r0 in range(0, x_ref.shape[0], ROW_SUBTILE):
        rows = slice(r0, r0 + ROW_SUBTILE)
        ub = (_ln_plain(x_ref[rows, :]) * (1.0 + mod[1:2]) + mod[0:1]).astype(BF16)

        def proj(lo, hi):
            base, w_ref = [sg for sg in segments if sg[0] <= lo][-1]
            return _dot(ub, w_ref[:, lo - base:hi - base])

        cos_t, sin_lo, sin_hi = rope_ref[0, rows, :], rope_ref[1, rows, :], rope_ref[2, rows, :]

        def rope(t):
            return (t * cos_t + pltpu.roll(t, HEAD_PAD - HALF_ROPE, 1) * sin_lo
                    + pltpu.roll(t, HALF_ROPE, 1) * sin_hi)

        ql = proj(OFF_QL, OFF_KVL)
        kvl = proj(OFF_KVL, OFF_KR)
        kr = proj(OFF_KR, N_IN_PAD)
        xr_ref[rows, :] = proj(0, D_MODEL)
        qn = _rms(ql, qn_ref[...]).astype(BF16)
        kvn = _rms(kvl, kvn_ref[...]).astype(BF16)
        kpe = rope(kr)
        gg_ref[rows, :] = jax.nn.gelu(proj(D_MODEL, 2 * D_MODEL)).astype(BF16)
        qf = _dot(qn, wq_ref[...])
        kf = _dot(kvn, wk_ref[...])
        vt_ref[:, rows] = lax.dot_general(wvt_ref[...], kvn, (((1,), (1,)), ((), ())),
                                          preferred_element_type=F32).astype(BF16)
        sa_ref[rows, :] = jax.nn.sigmoid(proj(2 * D_MODEL, 3 * D_MODEL)).astype(BF16)
        for h in range(N_HEADS):
            sl = slice(h * HEAD_PAD, (h + 1) * HEAD_PAD)
            q_ref[rows, sl] = rope(qf[:, sl]).astype(BF16)
            k_ref[rows, sl] = (kf[:, sl] + kpe).astype(BF16)
        sb_ref[rows, :] = jax.nn.sigmoid(proj(3 * D_MODEL, 4 * D_MODEL)).astype(BF16)


def _inproj(x, mod, rope_tab, w1s, qn, wq, kvn, wk, wv, tm):
    b, s, _ = x.shape
    tok = lambda w: pl.BlockSpec((None, tm, w), lambda bi, i: (bi, i, 0))
    shp = lambda w, dt: jax.ShapeDtypeStruct((b, s, w), dt)
    return pl.pallas_call(
        _inproj_kernel,
        grid=(b, s // tm),
        in_specs=[tok(D_MODEL),
                  pl.BlockSpec((None, 8, D_MODEL), lambda bi, i: (bi, 0, 0)),
                  pl.BlockSpec((3, tm, HEAD_PAD), lambda bi, i: (0, i, 0)),
                  *[_const_spec(w.shape) for w in w1s], _const_spec(qn.shape), _const_spec(wq.shape),
                  _const_spec(kvn.shape), _const_spec(wk.shape), _const_spec(wv.shape)],
        out_specs=[tok(D_RNN), tok(D_RNN), tok(D_MODEL), tok(D_MODEL),
                   tok(N_HEADS * HEAD_PAD), tok(N_HEADS * HEAD_PAD),
                   pl.BlockSpec((None, N_HEADS * V_HEAD, tm), lambda bi, i: (bi, 0, i))],
        out_shape=[shp(D_RNN, F32), shp(D_RNN, BF16), shp(D_MODEL, BF16), shp(D_MODEL, BF16),
                   shp(N_HEADS * HEAD_PAD, BF16), shp(N_HEADS * HEAD_PAD, BF16),
                   jax.ShapeDtypeStruct((b, N_HEADS * V_HEAD, s), BF16)],
        compiler_params=_params(2),
        name="inproj",
    )(x, mod, rope_tab, *w1s, qn, wq, kvn, wk, wv)


def _lru_kernel(xf_ref, xfp_ref, xfn_ref, xb_ref, xbp_ref, xbn_ref,
                cw_ref, cb_ref, wg_ref, br_ref, bi_ref, lam_ref,
                hf_ref, hb_ref, xs_ref, a_ref, u_ref, carry_ref, *, ts):
    c = pl.program_id(1)
    last = pl.num_programs(1) - 1

    @pl.when(c == 0)
    def _():
        carry_ref[...] = jnp.zeros_like(carry_ref)

    lam = lam_ref[...]
    sp = jnp.maximum(-lam, 0.0) + jnp.log1p(jnp.exp(-jnp.abs(lam)))
    half_c = (-0.5 * LRU_C) * sp
    cw = cw_ref[...]
    row = lax.broadcasted_iota(jnp.int32, (SUBLANES, D_RNN), 0)
    n_grp = ts // SUBLANES

    def gates(d, cur_ref, prev_ref, next_ref, keep_prev, keep_next):
        xs_ref[0:SUBLANES, :] = prev_ref[...] * keep_prev
        xs_ref[SUBLANES:SUBLANES + ts, :] = cur_ref[...]
        xs_ref[SUBLANES + ts:, :] = next_ref[...] * keep_next
        xc = cb_ref[...] + cw[0:1] * xs_ref[pl.ds(SUBLANES - RNN_CONV_LEFT, ts), :]
        for k in range(1, RNN_CONV):
            xc = xc + cw[k:k + 1] * xs_ref[pl.ds(SUBLANES - RNN_CONV_LEFT + k, ts), :]
        xcb = xc.astype(BF16)
        for g in range(N_GATE_GROUPS):
            sl = slice(g * GATE_GROUP, (g + 1) * GATE_GROUP)
            pre = _dot(xcb[:, sl], wg_ref[d, g])
            tr = jnp.tanh(0.5 * (pre[:, :GATE_GROUP] + br_ref[d:d + 1, sl]))
            ti = jnp.tanh(0.5 * (pre[:, GATE_GROUP:] + bi_ref[d:d + 1, sl]))
            log_a = tr * half_c[d:d + 1, sl] + half_c[d:d + 1, sl]
            a = jnp.exp(log_a)
            a_ref[d, :, sl] = a
            z = -jnp.tanh(log_a) * (a * a + 1.0)
            mult = jnp.where(z > 0.0, z * lax.rsqrt(z), 0.0)
            u_ref[d, :, sl] = mult * (0.5 * ti + 0.5) * xc[:, sl]

    def group(d, out_ref, g, carry):
        off = pl.multiple_of(g * SUBLANES, SUBLANES)
        a8 = a_ref[d, pl.ds(off, SUBLANES), :]
        u8 = u_ref[d, pl.ds(off, SUBLANES), :]
        entry = 0 if d == 0 else SUBLANES - 1
        u8 = u8 + jnp.where(row == entry, a8 * carry, 0.0)
        for sh in (1, 2, 4):
            if d == 0:
                edge, shift = row < sh, sh
            else:
                edge, shift = row >= SUBLANES - sh, SUBLANES - sh
            u8 = a8 * jnp.where(edge, 0.0, pltpu.roll(u8, shift, 0)) + u8
            if sh < 4:
                a8 = a8 * jnp.where(edge, 1.0, pltpu.roll(a8, shift, 0))
        out_ref[pl.ds(off, SUBLANES), :] = u8
        exit_ = SUBLANES - 1 - entry
        return u8[exit_:exit_ + 1, :]

    def scan_body(j, carry):
        return group(0, hf_ref, j, carry[0]), group(1, hb_ref, n_grp - 1 - j, carry[1])

    not_first = (c > 0).astype(F32)
    not_last = (c < last).astype(F32)
    gates(0, xf_ref, xfp_ref, xfn_ref, not_first, not_last)
    gates(1, xb_ref, xbp_ref, xbn_ref, not_last, not_first)
    cf, cb = lax.fori_loop(0, n_grp, scan_body, (carry_ref[0:1, :], carry_ref[1:2, :]))
    carry_ref[0:1, :] = cf
    carry_ref[1:2, :] = cb


def _lru(xr, cw, cb, wg, br, bi, lam, ts):
    b, s, _ = xr.shape
    nc = s // ts
    hpb = ts // SUBLANES
    nhb = s // SUBLANES
    cur = lambda f: pl.BlockSpec((None, ts, D_RNN), lambda bi_, c: (bi_, f(c), 0))
    halo = lambda f: pl.BlockSpec((None, SUBLANES, D_RNN), lambda bi_, c: (bi_, f(c), 0))
    fwd = lambda c: c
    bwd = lambda c: nc - 1 - c
    prev = lambda f: (lambda c: jnp.maximum(f(c) * hpb - 1, 0))
    nxt = lambda f: (lambda c: jnp.minimum((f(c) + 1) * hpb, nhb - 1))
    return pl.pallas_call(
        functools.partial(_lru_kernel, ts=ts),
        grid=(b, nc),
        in_specs=[cur(fwd), halo(prev(fwd)), halo(nxt(fwd)),
                  cur(bwd), halo(prev(bwd)), halo(nxt(bwd)),
                  _const_spec(cw.shape), _const_spec(cb.shape), _const_spec(wg.shape),
                  _const_spec(br.shape), _const_spec(bi.shape), _const_spec(lam.shape)],
        out_specs=[cur(fwd), cur(bwd)],
        out_shape=[jax.ShapeDtypeStruct((b, s, D_RNN), F32)] * 2,
        scratch_shapes=[pltpu.VMEM((ts + 2 * SUBLANES, D_RNN), F32),
                        pltpu.VMEM((2, ts, D_RNN), F32),
                        pltpu.VMEM((2, ts, D_RNN), F32),
                        pltpu.VMEM((SUBLANES, D_RNN), F32)],
        compiler_params=_params(2),
        name="lru",
    )(xr, xr, xr, xr, xr, xr, cw, cb, wg, br, bi, lam)


def _attn_kernel(q_ref, k_ref, vt_ref, o_ref, st_ref, *, tq, tk):
    s_len = k_ref.shape[0]
    n_kv, n_q = s_len // tk, s_len // tq
    total = n_q * n_kv
    unroll = min(ATTN_UNROLL, total)
    assert unroll % 2 == 0 and total % unroll == 0 and n_kv & (n_kv - 1) == 0
    per_tile = max(n_kv // unroll, 1)
    tiles_per_body = max(unroll // n_kv, 1)
    assert per_tile & (per_tile - 1) == 0
    heads = [slice(hh * HEAD_PAD, (hh + 1) * HEAD_PAD) for hh in range(2)]

    def scores(t, hh, slot):
        t = jnp.minimum(t, total - 1)
        i = lax.shift_right_logical(t, n_kv.bit_length() - 1)
        j = jnp.bitwise_and(t, n_kv - 1)
        q = q_ref[pl.ds(pl.multiple_of(i * tq, tq), tq), heads[hh]]
        k = k_ref[pl.ds(pl.multiple_of(j * tk, tk), tk), heads[hh]]
        st = lax.dot_general(k, q, (((1,), (1,)), ((), ())), preferred_element_type=F32)
        st_ref[slot, hh] = st
        return jnp.max(st, axis=0, keepdims=True)

    ones = jnp.ones((BF16_ROWS, MXU_DIM), BF16)

    def softmax_pv(j, hh, slot, m_old, acc, cmax):
        off = j * tk if isinstance(j, int) else pl.multiple_of(j * tk, tk)
        m_new = jnp.maximum(m_old, cmax)
        acc = jnp.exp2(m_old - m_new) * acc
        for t in range(tk // MXU_DIM):
            rows = slice(t * MXU_DIM, (t + 1) * MXU_DIM)
            pt = jnp.exp2(st_ref[slot, hh, rows, :] - m_new).astype(BF16)
            vt = vt_ref[hh * V_HEAD:(hh + 1) * V_HEAD, pl.ds(off + t * MXU_DIM, MXU_DIM)]
            acc = acc + _dot(jnp.concatenate([vt, ones], axis=0), pt)
        return m_new, acc

    def finish(i, carry):
        ot = jnp.concatenate([acc[:V_HEAD] / acc[V_HEAD:V_HEAD + 1] for _, acc, _ in carry], axis=0)
        row0 = i * tq if isinstance(i, int) else pl.multiple_of(i * tq, tq)
        o_ref[pl.ds(row0, tq), :] = ot.T.astype(o_ref.dtype)

    def body(bb, carry):
        jb = jnp.bitwise_and(bb, per_tile - 1)
        for step in range(unroll):
            slot = step % 2
            j = jb * unroll + step if tiles_per_body == 1 else step % n_kv
            new = []
            for hh in range(2):
                m_old, acc, cmax = carry[hh]
                cmax_next = scores(bb * unroll + step + 1, hh, 1 - slot)
                new.append(softmax_pv(j, hh, slot, m_old, acc, cmax) + (cmax_next,))
            carry = tuple(new)
            if tiles_per_body > 1 and (step + 1) % n_kv == 0:
                finish(bb * tiles_per_body + step // n_kv, carry)
                carry = tuple((m0, acc0, cmax) for _, _, cmax in carry)
        if tiles_per_body > 1:
            return carry
        i = lax.shift_right_logical(bb, per_tile.bit_length() - 1)
        if per_tile == 1:
            finish(i, carry)
            return tuple((m0, acc0, cmax) for _, _, cmax in carry)
        tile_done = jb == per_tile - 1
        pl.when(tile_done)(lambda: finish(i, carry))
        return tuple((jnp.where(tile_done, m0, m), jnp.where(tile_done, acc0, acc), cmax)
                     for m, acc, cmax in carry)

    m0 = jnp.full((1, tq), -jnp.inf, F32)
    acc0 = jnp.zeros((V_HEAD + BF16_ROWS, tq), F32)
    lax.fori_loop(0, total // unroll, body, tuple((m0, acc0, scores(0, hh, 0)) for hh in range(2)))


def _attn(q, k, vt, tq, tk):
    b, s, _ = q.shape
    resident = s * (4 * HEAD_PAD + 2 * V_HEAD) * 2
    mode = {} if 2 * resident <= VMEM_LIMIT // 4 else {"pipeline_mode": pl.Buffered(1)}
    return pl.pallas_call(
        functools.partial(_attn_kernel, tq=tq, tk=tk),
        scratch_shapes=[pltpu.VMEM((2, 2, tk, tq), F32)],
        grid=(b, N_HEADS // 2),
        in_specs=[pl.BlockSpec((None, s, 2 * HEAD_PAD), lambda bi, p: (bi, 0, p), **mode),
                  pl.BlockSpec((None, s, 2 * HEAD_PAD), lambda bi, p: (bi, 0, p), **mode),
                  pl.BlockSpec((None, 2 * V_HEAD, s), lambda bi, p: (bi, p, 0), **mode)],
        out_specs=pl.BlockSpec((None, s, 2 * V_HEAD), lambda bi, p: (bi, 0, p)),
        out_shape=jax.ShapeDtypeStruct((b, s, N_HEADS * V_HEAD), BF16),
        compiler_params=_params(2),
        name="attn",
    )(q, k, vt)


def _merge_kernel(x_ref, hf_ref, hb_ref, gg_ref, sa_ref, sb_ref, o_ref, mod_ref,
                  wr_ref, wm_ref, wo_ref, g_ref, b_ref, x1_ref, u2_ref):
    mod = mod_ref[...]
    for r0 in range(0, x_ref.shape[0], ROW_SUBTILE):
        rows = slice(r0, r0 + ROW_SUBTILE)
        yb = _dot(o_ref[rows, :], wm_ref[...])
        ya = _dot((gg_ref[rows, :] * (hf_ref[rows, :] + hb_ref[rows, :])).astype(BF16), wr_ref[...])
        mix = _dot((sa_ref[rows, :] * ya + sb_ref[rows, :] * yb).astype(BF16), wo_ref[...])
        x1 = _ln_plain(DN_ALPHA * x_ref[rows, :] + mod[2:3] * mix) * g_ref[...] + b_ref[...]
        x1_ref[rows, :] = x1
        u2_ref[rows, :] = (_ln_plain(x1) * (1.0 + mod[4:5]) + mod[3:4]).astype(BF16)


def _merge(x, hf, hb, gg, sa, sb, o, mod, wr, wm, wo, g1, b1, tm):
    b, s, _ = x.shape
    tok = pl.BlockSpec((None, tm, D_MODEL), lambda bi, i: (bi, i, 0))
    return pl.pallas_call(
        _merge_kernel,
        grid=(b, s // tm),
        in_specs=[tok] * 7 + [pl.BlockSpec((None, 8, D_MODEL), lambda bi, i: (bi, 0, 0)),
                              _const_spec(wr.shape), _const_spec(wm.shape), _const_spec(wo.shape),
                              _const_spec(g1.shape), _const_spec(b1.shape)],
        out_specs=[tok, tok],
        out_shape=[jax.ShapeDtypeStruct((b, s, D_MODEL), F32),
                   jax.ShapeDtypeStruct((b, s, D_MODEL), BF16)],
        compiler_params=_params(2),
        name="merge",
    )(x, hf, hb, gg, sa, sb, o, mod, wr, wm, wo, g1, b1)


def _ffn_kernel(u_ref, up_ref, un_ref, x1_ref, mod_ref, wv_ref, wg_ref, cw_ref, cb_ref, wd_ref,
                g_ref, b_ref, y_ref, ue_ref, hv_ref, hg_ref, act_ref, *, tm):
    i = pl.program_id(1)
    last = pl.num_programs(1) - 1
    halo = BF16_ROWS
    keep_prev = (i > 0).astype(BF16)
    keep_next = (i < last).astype(BF16)
    ue_ref[0:halo, :] = up_ref[...] * keep_prev
    ue_ref[halo:halo + tm, :] = u_ref[...]
    ue_ref[halo + tm:, :] = un_ref[...] * keep_next
    ue = ue_ref[...]

    def up(j):
        sl = slice(j * FF_CHUNK, (j + 1) * FF_CHUNK)
        hv_ref[j % 2] = _dot(ue, wv_ref[:, sl])
        hg_ref[j % 2] = _dot(ue, wg_ref[:, sl])

    def conv(h_ref, slot, w, bias):
        return (bias + w[0:1] * h_ref[slot, pl.ds(halo - 1, tm), :]
                + w[1:2] * h_ref[slot, pl.ds(halo, tm), :]
                + w[2:3] * h_ref[slot, pl.ds(halo + 1, tm), :])

    up(0)
    for j in range(N_FF_CHUNKS):
        if j + 1 < N_FF_CHUNKS:
            up(j + 1)
        sl = slice(j * FF_CHUNK, (j + 1) * FF_CHUNK)
        slg = slice(D_FF + j * FF_CHUNK, D_FF + (j + 1) * FF_CHUNK)
        cv = conv(hv_ref, j % 2, cw_ref[:, sl], cb_ref[:, sl])
        cg = conv(hg_ref, j % 2, cw_ref[:, slg], cb_ref[:, slg])
        act_ref[:, sl] = (cg * jax.nn.sigmoid(cg) * cv).astype(BF16)
    y = _dot(act_ref[...], wd_ref[...])
    mod = mod_ref[...]
    y_ref[...] = _ln_plain(DN_ALPHA * x1_ref[...] + mod[5:6] * y) * g_ref[...] + b_ref[...]


def _ffn(u2, x1, mod, wv, wg, cw, cb, wd, g2, b2, tm):
    b, s, _ = x1.shape
    hpb = tm // BF16_ROWS
    nhb = s // BF16_ROWS
    tok = pl.BlockSpec((None, tm, D_MODEL), lambda bi, i: (bi, i, 0))
    return pl.pallas_call(
        functools.partial(_ffn_kernel, tm=tm),
        grid=(b, s // tm),
        in_specs=[tok,
                  pl.BlockSpec((None, BF16_ROWS, D_MODEL),
                               lambda bi, i: (bi, jnp.maximum(i * hpb - 1, 0), 0)),
                  pl.BlockSpec((None, BF16_ROWS, D_MODEL),
                               lambda bi, i: (bi, jnp.minimum((i + 1) * hpb, nhb - 1), 0)),
                  tok,
                  pl.BlockSpec((None, 8, D_MODEL), lambda bi, i: (bi, 0, 0)),
                  _const_spec(wv.shape), _const_spec(wg.shape), _const_spec(cw.shape),
                  _const_spec(cb.shape), _const_spec(wd.shape), _const_spec(g2.shape),
                  _const_spec(b2.shape)],
        out_specs=tok,
        out_shape=jax.ShapeDtypeStruct((b, s, D_MODEL), F32),
        scratch_shapes=[pltpu.VMEM((tm + 2 * BF16_ROWS, D_MODEL), BF16),
                        pltpu.VMEM((2, tm + 2 * BF16_ROWS, FF_CHUNK), F32),
                        pltpu.VMEM((2, tm + 2 * BF16_ROWS, FF_CHUNK), F32),
                        pltpu.VMEM((tm, D_FF), BF16)],
        compiler_params=_params(2),
        name="ffn",
    )(u2, u2, u2, x1, mod, wv, wg, cw, cb, wd, g2, b2)


def _rope_table(s):
    inv = 1.0 / (ROPE_BASE ** (jnp.arange(0, QK_ROPE, 2, dtype=F32) / QK_ROPE))
    ang = jnp.arange(s, dtype=F32)[:, None] * inv[None, :]
    cos, sin = jnp.cos(ang), jnp.sin(ang)
    one = jnp.ones((s, QK_NOPE), F32)
    zero = jnp.zeros((s, HALF_ROPE), F32)
    tail = jnp.zeros((s, HEAD_PAD - QK_NOPE - QK_ROPE), F32)
    zn = jnp.zeros((s, QK_NOPE), F32)
    cos_t = jnp.concatenate([one, cos, cos, tail], axis=1)
    sin_lo = jnp.concatenate([zn, -sin, zero, tail], axis=1)
    sin_hi = jnp.concatenate([zn, zero, sin, tail], axis=1)
    return jnp.stack([cos_t, sin_lo, sin_hi])


def _prep_weights(w_in, lru_w_r, lru_w_i, w_q_b, w_kv_b, w_up):
    o1, o2 = D_RNN, 2 * D_RNN
    o3, o4, o5 = o2 + Q_LORA, o2 + Q_LORA + KV_LORA, o2 + Q_LORA + KV_LORA + QK_ROPE
    o6 = o5 + D_MODEL
    kr = jnp.pad(w_in[:, o4:o5], ((0, 0), (QK_NOPE, HEAD_PAD - QK_NOPE - QK_ROPE)))
    w1 = (w_in[:, :o2].astype(BF16), w_in[:, o5:].astype(BF16),
          jnp.concatenate([w_in[:, o2:o4], kr], axis=1).astype(BF16))
    eye = jnp.eye(RNN_BLOCKS // N_GATE_GROUPS, dtype=F32)

    def blockdiag(w):
        wgp = w.reshape(2, N_GATE_GROUPS, RNN_BLOCKS // N_GATE_GROUPS, RNN_BLOCK, RNN_BLOCK)
        return jnp.einsum("dgnkj,nm->dgnkmj", wgp, eye).reshape(2, N_GATE_GROUPS, GATE_GROUP, GATE_GROUP)

    wgate = jnp.concatenate([blockdiag(lru_w_r), blockdiag(lru_w_i)], axis=-1).astype(BF16)
    qscale = (QK_NOPE + QK_ROPE) ** -0.5 * math.log2(math.e)
    wq = jnp.pad((w_q_b * qscale).reshape(Q_LORA, N_HEADS, QK_NOPE + QK_ROPE),
                 ((0, 0), (0, 0), (0, HEAD_PAD - QK_NOPE - QK_ROPE)))
    wq = wq.reshape(Q_LORA, N_HEADS * HEAD_PAD).astype(BF16)
    wkv = w_kv_b.reshape(KV_LORA, N_HEADS, QK_NOPE + V_HEAD)
    wk = jnp.pad(wkv[:, :, :QK_NOPE], ((0, 0), (0, 0), (0, HEAD_PAD - QK_NOPE)))
    wk = wk.reshape(KV_LORA, N_HEADS * HEAD_PAD).astype(BF16)
    wvt = wkv[:, :, QK_NOPE:].reshape(KV_LORA, N_HEADS * V_HEAD).T.astype(BF16)
    return w1, wgate, wq, wk, wvt, w_up[:, :D_FF].astype(BF16), w_up[:, D_FF:].astype(BF16)


def _tile(s, pref):
    return min(s, pref)


def _layer(x, mod, p, *, tm_in=512, ts=256, tq=256, tk=2048, tm_merge=512, tm_ffn=512):
    s = x.shape[1]
    xr, gg, sa, sb, q, k, vt = _inproj(x, mod, p["rope_tab"], p["w1"], p["q_norm"], p["wq"], p["kv_norm"],
                                       p["wk"], p["wvt"], _tile(s, tm_in))
    hf, hb = _lru(xr, p["rnn_conv_w"], p["rnn_conv_b"], p["wgate"], p["lru_b_r"], p["lru_b_i"],
                  p["lru_lam"], _tile(s, ts))
    o = _attn(q, k, vt, _tile(s, tq), _tile(s // 2, tk))
    x1, u2 = _merge(x, hf, hb, gg, sa, sb, o, mod, p["w_rnn_proj"], p["w_mla_proj"], p["w_out"],
                    p["ln1_g"], p["ln1_b"], _tile(s, tm_merge))
    return _ffn(u2, x1, mod, p["w_up_v"], p["w_up_g"], p["ffn_conv_w"], p["ffn_conv_b"], p["w_down"],
                p["ln2_g"], p["ln2_b"], _tile(s, tm_ffn))


def kernel(x_prompt, x_sample, c_prompt, c_sample, w_ada, b_ada, w_in, rnn_conv_w, rnn_conv_b, lru_w_r, lru_b_r, lru_w_i, lru_b_i, lru_lam, w_rnn_proj, q_norm, w_q_b, kv_norm, w_kv_b, w_mla_proj, w_out, ln1_g, ln1_b, w_up, ffn_conv_w, ffn_conv_b, w_down, ln2_g, ln2_b):
    assert w_ada.shape[0] == DEPTH
    l = 0
    nb_p, nb_s = c_prompt.shape[0], c_sample.shape[0]
    rows = -(-(nb_p + nb_s) // SUBLANES) * SUBLANES
    c_all = jnp.concatenate([c_prompt, c_sample, jnp.zeros((rows - nb_p - nb_s, D_MODEL), F32)])
    mod = _ada(c_all, w_ada[l], b_ada[l][None, :]).reshape(rows, 6, D_MODEL)
    mod = jnp.pad(mod, ((0, 0), (0, 2), (0, 0)))
    w1, wgate, wq, wk, wvt, w_up_v, w_up_g = _prep_weights(w_in[l], lru_w_r[l], lru_w_i[l], w_q_b[l],
                                                         w_kv_b[l], w_up[l])
    row = lambda a: a[l][None, :]
    p = {
        "rope_tab": _rope_table(max(x_prompt.shape[1], x_sample.shape[1])),
        "w1": w1, "wgate": wgate, "wq": wq, "wk": wk, "wvt": wvt, "w_up_v": w_up_v, "w_up_g": w_up_g,
        "q_norm": row(q_norm), "kv_norm": row(kv_norm),
        "rnn_conv_w": rnn_conv_w[l], "rnn_conv_b": row(rnn_conv_b),
        "lru_b_r": lru_b_r[l], "lru_b_i": lru_b_i[l], "lru_lam": lru_lam[l],
        "w_rnn_proj": w_rnn_proj[l].astype(BF16), "w_mla_proj": w_mla_proj[l].astype(BF16),
        "w_out": w_out[l].astype(BF16), "ln1_g": row(ln1_g), "ln1_b": row(ln1_b),
        "ffn_conv_w": ffn_conv_w[l], "ffn_conv_b": row(ffn_conv_b),
        "w_down": w_down[l].astype(BF16), "ln2_g": row(ln2_g), "ln2_b": row(ln2_b),
    }
    y_prompt = _layer(x_prompt, mod[:nb_p], p)
    y_sample = _layer(x_sample, mod[nb_p:nb_p + nb_s], p)
    return (y_prompt, y_sample)
```

```python
import functools
import math

import jax
import jax.numpy as jnp
from jax import lax
from jax.experimental import pallas as pl
from jax.experimental.pallas import tpu as pltpu

F32 = jnp.float32
BF16 = jnp.bfloat16

D_MODEL = 1024
D_RNN = 1024
RNN_BLOCKS = 16
RNN_BLOCK = D_RNN // RNN_BLOCKS
RNN_CONV = 4
RNN_CONV_LEFT = 2
LRU_C = 8.0
N_HEADS = 16
QK_NOPE = 64
QK_ROPE = 32
V_HEAD = 64
Q_LORA = 384
KV_LORA = 256
ROPE_BASE = 10000.0
D_FF = 2816
FFN_CONV = 3
LN_EPS = 1e-5
RMS_EPS = 1e-6
DEPTH = 1
DN_ALPHA = (2.0 * DEPTH) ** 0.25

LANES = 128
SUBLANES = 8
BF16_ROWS = 16
MXU_DIM = 256
VMEM_LIMIT = 56 * 1024 * 1024

HEAD_PAD = LANES
HALF_ROPE = QK_ROPE // 2
GATE_GROUP = MXU_DIM
N_GATE_GROUPS = D_RNN // GATE_GROUP
ROW_SUBTILE = MXU_DIM
ATTN_UNROLL = 8
FF_CHUNK = MXU_DIM
N_FF_CHUNKS = D_FF // FF_CHUNK
OFF_QL = 4 * D_MODEL
OFF_KVL = OFF_QL + Q_LORA
OFF_KR = OFF_KVL + KV_LORA
N_IN_PAD = OFF_KR + HEAD_PAD


def _const_spec(shape):
    nd = len(shape)
    return pl.BlockSpec(shape, lambda *_: (0,) * nd, pipeline_mode=pl.Buffered(1))


def _params(n_grid):
    return pltpu.CompilerParams(dimension_semantics=("arbitrary",) * n_grid,
                                vmem_limit_bytes=VMEM_LIMIT)


def _ln_plain(x):
    mu = jnp.mean(x, axis=-1, keepdims=True)
    xc = x - mu
    var = jnp.mean(xc * xc, axis=-1, keepdims=True)
    return xc * lax.rsqrt(var + LN_EPS)


def _rms(x, g):
    return x * lax.rsqrt(jnp.mean(x * x, axis=-1, keepdims=True) + RMS_EPS) * g


def _dot(a, b):
    return jnp.dot(a, b, preferred_element_type=F32)


def _ada_kernel(c_ref, w_ref, b_ref, o_ref):
    c = c_ref[...]
    s = (c * jax.nn.sigmoid(c)).astype(BF16)
    o_ref[...] = _dot(s, w_ref[...].astype(BF16)) + b_ref[...]


def _ada(c_all, w_ada, b_ada, tn=512):
    rows, n = c_all.shape[0], w_ada.shape[1]
    return pl.pallas_call(
        _ada_kernel,
        grid=(n // tn,),
        in_specs=[pl.BlockSpec((rows, D_MODEL), lambda j: (0, 0)),
                  pl.BlockSpec((D_MODEL, tn), lambda j: (0, j)),
                  pl.BlockSpec((1, tn), lambda j: (0, j))],
        out_specs=pl.BlockSpec((rows, tn), lambda j: (0, j)),
        out_shape=jax.ShapeDtypeStruct((rows, n), F32),
        compiler_params=_params(1),
        name="ada",
    )(c_all, w_ada, b_ada)


def _inproj_kernel(x_ref, xp_ref, xn_ref, mod_ref, rope_ref, w1_ref, qn_ref, wq_ref, kvn_ref, wk_ref,
                   wvt_ref, cw_ref, cb_ref, xc_ref, gg_ref, sa_ref, sb_ref, q_ref, k_ref, vt_ref, xs_ref):
    i = pl.program_id(1)
    last = pl.num_programs(1) - 1
    tm = x_ref.shape[0]
    mod = mod_ref[...]
    cw = cw_ref[...]

    def modulated(x):
        return (_ln_plain(x) * (1.0 + mod[1:2]) + mod[0:1]).astype(BF16)

    for n, r0 in enumerate(range(0, tm, ROW_SUBTILE)):
        rows = slice(r0, r0 + ROW_SUBTILE)
        ub = modulated(x_ref[rows, :])

        def proj(lo, hi):
            return _dot(ub, w1_ref[:, lo:hi])

        x_prev = x_ref[r0 - SUBLANES:r0, :] if r0 > 0 else xp_ref[...]
        x_next = x_ref[r0 + ROW_SUBTILE:r0 + ROW_SUBTILE + SUBLANES, :] if r0 + ROW_SUBTILE < tm else xn_ref[...]
        halo = _dot(modulated(jnp.concatenate([x_prev, x_next], axis=0)), w1_ref[:, 0:D_RNN])
        keep_prev = 1.0 if r0 > 0 else (i > 0).astype(F32)
        keep_next = 1.0 if r0 + ROW_SUBTILE < tm else (i < last).astype(F32)
        xs_ref[n, 0:SUBLANES, :] = halo[0:SUBLANES] * keep_prev
        xs_ref[n, SUBLANES + ROW_SUBTILE:, :] = halo[SUBLANES:] * keep_next

        cos_t, sin_lo, sin_hi = rope_ref[0, rows, :], rope_ref[1, rows, :], rope_ref[2, rows, :]

        def rope(t):
            return (t * cos_t + pltpu.roll(t, HEAD_PAD - HALF_ROPE, 1) * sin_lo
                    + pltpu.roll(t, HALF_ROPE, 1) * sin_hi)

        ql = proj(OFF_QL, OFF_KVL)
        kvl = proj(OFF_KVL, OFF_KR)
        kr = proj(OFF_KR, N_IN_PAD)
        xs_ref[n, SUBLANES:SUBLANES + ROW_SUBTILE, :] = proj(0, D_RNN)
        qn = _rms(ql, qn_ref[...]).astype(BF16)
        kvn = _rms(kvl, kvn_ref[...]).astype(BF16)
        kpe = rope(kr)
        gg_ref[rows, :] = jax.nn.gelu(proj(D_MODEL, 2 * D_MODEL)).astype(BF16)
        qf = _dot(qn, wq_ref[...])
        kf = _dot(kvn, wk_ref[...])
        vt_ref[:, rows] = lax.dot_general(wvt_ref[...], kvn, (((1,), (1,)), ((), ())),
                                          preferred_element_type=F32).astype(BF16)
        sa_ref[rows, :] = jax.nn.sigmoid(proj(2 * D_MODEL, 3 * D_MODEL)).astype(BF16)
        xc = cb_ref[...] + cw[0:1] * xs_ref[n, pl.ds(SUBLANES - RNN_CONV_LEFT, ROW_SUBTILE), :]
        for kk in range(1, RNN_CONV):
            xc = xc + cw[kk:kk + 1] * xs_ref[n, pl.ds(SUBLANES - RNN_CONV_LEFT + kk, ROW_SUBTILE), :]
        xc_ref[rows, :] = xc
        for h in range(N_HEADS):
            sl = slice(h * HEAD_PAD, (h + 1) * HEAD_PAD)
            q_ref[rows, sl] = rope(qf[:, sl]).astype(BF16)
            k_ref[rows, sl] = (kf[:, sl] + kpe).astype(BF16)
        sb_ref[rows, :] = jax.nn.sigmoid(proj(3 * D_MODEL, 4 * D_MODEL)).astype(BF16)


def _inproj(x, mod, rope_tab, w1, qn, wq, kvn, wk, wv, cw, cb, tm):
    b, s, _ = x.shape
    hpb = tm // SUBLANES
    nhb = s // SUBLANES
    tok = lambda w: pl.BlockSpec((None, tm, w), lambda bi, i: (bi, i, 0))
    halo = lambda f: pl.BlockSpec((None, SUBLANES, D_MODEL), lambda bi, i: (bi, f(i), 0))
    shp = lambda w, dt: jax.ShapeDtypeStruct((b, s, w), dt)
    return pl.pallas_call(
        _inproj_kernel,
        grid=(b, s // tm),
        in_specs=[tok(D_MODEL),
                  halo(lambda i: jnp.maximum(i * hpb - 1, 0)),
                  halo(lambda i: jnp.minimum((i + 1) * hpb, nhb - 1)),
                  pl.BlockSpec((None, 8, D_MODEL), lambda bi, i: (bi, 0, 0)),
                  pl.BlockSpec((3, tm, HEAD_PAD), lambda bi, i: (0, i, 0)),
                  _const_spec(w1.shape), _const_spec(qn.shape), _const_spec(wq.shape),
                  _const_spec(kvn.shape), _const_spec(wk.shape), _const_spec(wv.shape),
                  _const_spec(cw.shape), _const_spec(cb.shape)],
        scratch_shapes=[pltpu.VMEM((tm // ROW_SUBTILE, ROW_SUBTILE + 2 * SUBLANES, D_RNN), F32)],
        out_specs=[tok(D_RNN), tok(D_RNN), tok(D_MODEL), tok(D_MODEL),
                   tok(N_HEADS * HEAD_PAD), tok(N_HEADS * HEAD_PAD),
                   pl.BlockSpec((None, N_HEADS * V_HEAD, tm), lambda bi, i: (bi, 0, i))],
        out_shape=[shp(D_RNN, F32), shp(D_RNN, BF16), shp(D_MODEL, BF16), shp(D_MODEL, BF16),
                   shp(N_HEADS * HEAD_PAD, BF16), shp(N_HEADS * HEAD_PAD, BF16),
                   jax.ShapeDtypeStruct((b, N_HEADS * V_HEAD, s), BF16)],
        compiler_params=_params(2),
        name="inproj",
    )(x, x, x, mod, rope_tab, w1, qn, wq, kvn, wk, wv, cw, cb)


def _lru_kernel(xf_ref, xb_ref, wg_ref, br_ref, bi_ref, lam_ref,
                hf_ref, hb_ref, a_ref, u_ref, carry_ref, *, ts):
    @pl.when(pl.program_id(1) == 0)
    def _():
        carry_ref[...] = jnp.zeros_like(carry_ref)

    lam = lam_ref[...]
    sp = jnp.maximum(-lam, 0.0) + jnp.log1p(jnp.exp(-jnp.abs(lam)))
    half_c = (-0.5 * LRU_C) * sp
    row = lax.broadcasted_iota(jnp.int32, (SUBLANES, D_RNN), 0)
    n_grp = ts // SUBLANES

    def gates(d, cur_ref):
        xc = cur_ref[...]
        xcb = xc.astype(BF16)
        for g in range(N_GATE_GROUPS):
            sl = slice(g * GATE_GROUP, (g + 1) * GATE_GROUP)
            pre = _dot(xcb[:, sl], wg_ref[d, g])
            tr = jnp.tanh(0.5 * (pre[:, :GATE_GROUP] + br_ref[d:d + 1, sl]))
            ti = jnp.tanh(0.5 * (pre[:, GATE_GROUP:] + bi_ref[d:d + 1, sl]))
            log_a = tr * half_c[d:d + 1, sl] + half_c[d:d + 1, sl]
            a = jnp.exp(log_a)
            a_ref[d, :, sl] = a
            z = -jnp.tanh(log_a) * (a * a + 1.0)
            mult = jnp.where(z > 0.0, z * lax.rsqrt(z), 0.0)
            u_ref[d, :, sl] = mult * (0.5 * ti + 0.5) * xc[:, sl]

    def group(d, out_ref, g, carry):
        off = pl.multiple_of(g * SUBLANES, SUBLANES)
        a8 = a_ref[d, pl.ds(off, SUBLANES), :]
        u8 = u_ref[d, pl.ds(off, SUBLANES), :]
        entry = 0 if d == 0 else SUBLANES - 1
        u8 = u8 + jnp.where(row == entry, a8 * carry, 0.0)
        for sh in (1, 2, 4):
            if d == 0:
                edge, shift = row < sh, sh
            else:
                edge, shift = row >= SUBLANES - sh, SUBLANES - sh
            u8 = a8 * jnp.where(edge, 0.0, pltpu.roll(u8, shift, 0)) + u8
            if sh < 4:
                a8 = a8 * jnp.where(edge, 1.0, pltpu.roll(a8, shift, 0))
        out_ref[pl.ds(off, SUBLANES), :] = u8
        exit_ = SUBLANES - 1 - entry
        return u8[exit_:exit_ + 1, :]

    def scan_body(j, carry):
        return group(0, hf_ref, j, carry[0]), group(1, hb_ref, n_grp - 1 - j, carry[1])

    gates(0, xf_ref)
    gates(1, xb_ref)
    cf, cb = lax.fori_loop(0, n_grp, scan_body, (carry_ref[0:1, :], carry_ref[1:2, :]))
    carry_ref[0:1, :] = cf
    carry_ref[1:2, :] = cb


def _lru(xc, wg, br, bi, lam, ts):
    b, s, _ = xc.shape
    nc = s // ts
    fwd = pl.BlockSpec((None, ts, D_RNN), lambda bi_, c: (bi_, c, 0))
    bwd = pl.BlockSpec((None, ts, D_RNN), lambda bi_, c: (bi_, nc - 1 - c, 0))
    return pl.pallas_call(
        functools.partial(_lru_kernel, ts=ts),
        grid=(b, nc),
        in_specs=[fwd, bwd, _const_spec(wg.shape),
                  _const_spec(br.shape), _const_spec(bi.shape), _const_spec(lam.shape)],
        out_specs=[fwd, bwd],
        out_shape=[jax.ShapeDtypeStruct((b, s, D_RNN), F32)] * 2,
        scratch_shapes=[pltpu.VMEM((2, ts, D_RNN), F32),
                        pltpu.VMEM((2, ts, D_RNN), F32),
                        pltpu.VMEM((SUBLANES, D_RNN), F32)],
        compiler_params=_params(2),
        name="lru",
    )(xc, xc, wg, br, bi, lam)


def _attn_kernel(q_ref, k_ref, vt_ref, o_ref, st_ref, *, tq, tk):
    s_len = k_ref.shape[0]
    n_kv, n_q = s_len // tk, s_len // tq
    total = n_q * n_kv
    unroll = min(ATTN_UNROLL, total)
    assert unroll % 2 == 0 and total % unroll == 0 and n_kv & (n_kv - 1) == 0
    per_tile = max(n_kv // unroll, 1)
    tiles_per_body = max(unroll // n_kv, 1)
    assert per_tile & (per_tile - 1) == 0
    heads = [slice(hh * HEAD_PAD, (hh + 1) * HEAD_PAD) for hh in range(2)]

    def scores(t, hh, slot):
        t = jnp.minimum(t, total - 1)
        i = lax.shift_right_logical(t, n_kv.bit_length() - 1)
        j = jnp.bitwise_and(t, n_kv - 1)
        q = q_ref[pl.ds(pl.multiple_of(i * tq, tq), tq), heads[hh]]
        k = k_ref[pl.ds(pl.multiple_of(j * tk, tk), tk), heads[hh]]
        st = lax.dot_general(k, q, (((1,), (1,)), ((), ())), preferred_element_type=F32)
        st_ref[slot, hh] = st
        return jnp.max(st, axis=0, keepdims=True)

    ones = jnp.ones((BF16_ROWS, MXU_DIM), BF16)

    def softmax_pv(j, hh, slot, m_old, acc, cmax):
        off = j * tk if isinstance(j, int) else pl.multiple_of(j * tk, tk)
        m_new = jnp.maximum(m_old, cmax)
        acc = jnp.exp2(m_old - m_new) * acc
        for t in range(tk // MXU_DIM):
            rows = slice(t * MXU_DIM, (t + 1) * MXU_DIM)
            pt = jnp.exp2(st_ref[slot, hh, rows, :] - m_new).astype(BF16)
            vt = vt_ref[hh * V_HEAD:(hh + 1) * V_HEAD, pl.ds(off + t * MXU_DIM, MXU_DIM)]
            acc = acc + _dot(jnp.concatenate([vt, ones], axis=0), pt)
        return m_new, acc

    def finish(i, carry):
        ot = jnp.concatenate([acc[:V_HEAD] / acc[V_HEAD:V_HEAD + 1] for _, acc, _ in carry], axis=0)
        row0 = i * tq if isinstance(i, int) else pl.multiple_of(i * tq, tq)
        o_ref[pl.ds(row0, tq), :] = ot.T.astype(o_ref.dtype)

    def body(bb, carry):
        jb = jnp.bitwise_and(bb, per_tile - 1)
        for step in range(unroll):
            slot = step % 2
            j = jb * unroll + step if tiles_per_body == 1 else step % n_kv
            new = []
            for hh in range(2):
                m_old, acc, cmax = carry[hh]
                cmax_next = scores(bb * unroll + step + 1, hh, 1 - slot)
                new.append(softmax_pv(j, hh, slot, m_old, acc, cmax) + (cmax_next,))
            carry = tuple(new)
            if tiles_per_body > 1 and (step + 1) % n_kv == 0:
                finish(bb * tiles_per_body + step // n_kv, carry)
                carry = tuple((m0, acc0, cmax) for _, _, cmax in carry)
        if tiles_per_body > 1:
            return carry
        i = lax.shift_right_logical(bb, per_tile.bit_length() - 1)
        if per_tile == 1:
            finish(i, carry)
            return tuple((m0, acc0, cmax) for _, _, cmax in carry)
        tile_done = jb == per_tile - 1
        pl.when(tile_done)(lambda: finish(i, carry))
        return tuple((jnp.where(tile_done, m0, m), jnp.where(tile_done, acc0, acc), cmax)
                     for m, acc, cmax in carry)

    m0 = jnp.full((1, tq), -jnp.inf, F32)
    acc0 = jnp.zeros((V_HEAD + BF16_ROWS, tq), F32)
    lax.fori_loop(0, total // unroll, body, tuple((m0, acc0, scores(0, hh, 0)) for hh in range(2)))


def _attn(q, k, vt, tq, tk):
    b, s, _ = q.shape
    resident = s * (4 * HEAD_PAD + 2 * V_HEAD) * 2
    mode = {} if 2 * resident <= VMEM_LIMIT // 4 else {"pipeline_mode": pl.Buffered(1)}
    return pl.pallas_call(
        functools.partial(_attn_kernel, tq=tq, tk=tk),
        scratch_shapes=[pltpu.VMEM((2, 2, tk, tq), F32)],
        grid=(b, N_HEADS // 2),
        in_specs=[pl.BlockSpec((None, s, 2 * HEAD_PAD), lambda bi, p: (bi, 0, p), **mode),
                  pl.BlockSpec((None, s, 2 * HEAD_PAD), lambda bi, p: (bi, 0, p), **mode),
                  pl.BlockSpec((None, 2 * V_HEAD, s), lambda bi, p: (bi, p, 0), **mode)],
        out_specs=pl.BlockSpec((None, s, 2 * V_HEAD), lambda bi, p: (bi, 0, p)),
        out_shape=jax.ShapeDtypeStruct((b, s, N_HEADS * V_HEAD), BF16),
        compiler_params=_params(2),
        name="attn",
    )(q, k, vt)


def _merge_kernel(x_ref, hf_ref, hb_ref, gg_ref, sa_ref, sb_ref, o_ref, mod_ref,
                  wr_ref, wm_ref, wo_ref, g_ref, b_ref, x1_ref, u2_ref):
    mod = mod_ref[...]
    for r0 in range(0, x_ref.shape[0], ROW_SUBTILE):
        rows = slice(r0, r0 + ROW_SUBTILE)
        yb = _dot(o_ref[rows, :], wm_ref[...])
        ya = _dot((gg_ref[rows, :] * (hf_ref[rows, :] + hb_ref[rows, :])).astype(BF16), wr_ref[...])
        mix = _dot((sa_ref[rows, :] * ya + sb_ref[rows, :] * yb).astype(BF16), wo_ref[...])
        x1 = _ln_plain(DN_ALPHA * x_ref[rows, :] + mod[2:3] * mix) * g_ref[...] + b_ref[...]
        x1_ref[rows, :] = x1
        u2_ref[rows, :] = (_ln_plain(x1) * (1.0 + mod[4:5]) + mod[3:4]).astype(BF16)


def _merge(x, hf, hb, gg, sa, sb, o, mod, wr, wm, wo, g1, b1, tm):
    b, s, _ = x.shape
    tok = pl.BlockSpec((None, tm, D_MODEL), lambda bi, i: (bi, i, 0))
    return pl.pallas_call(
        _merge_kernel,
        grid=(b, s // tm),
        in_specs=[tok] * 7 + [pl.BlockSpec((None, 8, D_MODEL), lambda bi, i: (bi, 0, 0)),
                              _const_spec(wr.shape), _const_spec(wm.shape), _const_spec(wo.shape),
                              _const_spec(g1.shape), _const_spec(b1.shape)],
        out_specs=[tok, tok],
        out_shape=[jax.ShapeDtypeStruct((b, s, D_MODEL), F32),
                   jax.ShapeDtypeStruct((b, s, D_MODEL), BF16)],
        compiler_params=_params(2),
        name="merge",
    )(x, hf, hb, gg, sa, sb, o, mod, wr, wm, wo, g1, b1)


def _ffn_kernel(u_ref, up_ref, un_ref, x1_ref, mod_ref, wv_ref, wg_ref, cw_ref, cb_ref, wd_ref,
                g_ref, b_ref, y_ref, ue_ref, hv_ref, hg_ref, act_ref, *, tm):
    i = pl.program_id(1)
    last = pl.num_programs(1) - 1
    halo = BF16_ROWS
    keep_prev = (i > 0).astype(BF16)
    keep_next = (i < last).astype(BF16)
    ue_ref[0:halo, :] = up_ref[...] * keep_prev
    ue_ref[halo:halo + tm, :] = u_ref[...]
    ue_ref[halo + tm:, :] = un_ref[...] * keep_next
    ue = ue_ref[...]

    def up(j):
        sl = slice(j * FF_CHUNK, (j + 1) * FF_CHUNK)
        hv_ref[j % 2] = _dot(ue, wv_ref[:, sl])
        hg_ref[j % 2] = _dot(ue, wg_ref[:, sl])

    def conv(h_ref, slot, w, bias):
        return (bias + w[0:1] * h_ref[slot, pl.ds(halo - 1, tm), :]
                + w[1:2] * h_ref[slot, pl.ds(halo, tm), :]
                + w[2:3] * h_ref[slot, pl.ds(halo + 1, tm), :])

    up(0)
    for j in range(N_FF_CHUNKS):
        if j + 1 < N_FF_CHUNKS:
            up(j + 1)
        sl = slice(j * FF_CHUNK, (j + 1) * FF_CHUNK)
        slg = slice(D_FF + j * FF_CHUNK, D_FF + (j + 1) * FF_CHUNK)
        cv = conv(hv_ref, j % 2, cw_ref[:, sl], cb_ref[:, sl])
        cg = conv(hg_ref, j % 2, cw_ref[:, slg], cb_ref[:, slg])
        act_ref[:, sl] = (cg * jax.nn.sigmoid(cg) * cv).astype(BF16)
    y = _dot(act_ref[...], wd_ref[...])
    mod = mod_ref[...]
    y_ref[...] = _ln_plain(DN_ALPHA * x1_ref[...] + mod[5:6] * y) * g_ref[...] + b_ref[...]


def _ffn(u2, x1, mod, wv, wg, cw, cb, wd, g2, b2, tm):
    b, s, _ = x1.shape
    hpb = tm // BF16_ROWS
    nhb = s // BF16_ROWS
    tok = pl.BlockSpec((None, tm, D_MODEL), lambda bi, i: (bi, i, 0))
    return pl.pallas_call(
        functools.partial(_ffn_kernel, tm=tm),
        grid=(b, s // tm),
        in_specs=[tok,
                  pl.BlockSpec((None, BF16_ROWS, D_MODEL),
                               lambda bi, i: (bi, jnp.maximum(i * hpb - 1, 0), 0)),
                  pl.BlockSpec((None, BF16_ROWS, D_MODEL),
                               lambda bi, i: (bi, jnp.minimum((i + 1) * hpb, nhb - 1), 0)),
                  tok,
                  pl.BlockSpec((None, 8, D_MODEL), lambda bi, i: (bi, 0, 0)),
                  _const_spec(wv.shape), _const_spec(wg.shape), _const_spec(cw.shape),
                  _const_spec(cb.shape), _const_spec(wd.shape), _const_spec(g2.shape),
                  _const_spec(b2.shape)],
        out_specs=tok,
        out_shape=jax.ShapeDtypeStruct((b, s, D_MODEL), F32),
        scratch_shapes=[pltpu.VMEM((tm + 2 * BF16_ROWS, D_MODEL), BF16),
                        pltpu.VMEM((2, tm + 2 * BF16_ROWS, FF_CHUNK), F32),
                        pltpu.VMEM((2, tm + 2 * BF16_ROWS, FF_CHUNK), F32),
                        pltpu.VMEM((tm, D_FF), BF16)],
        compiler_params=_params(2),
        name="ffn",
    )(u2, u2, u2, x1, mod, wv, wg, cw, cb, wd, g2, b2)


def _rope_table(s):
    inv = 1.0 / (ROPE_BASE ** (jnp.arange(0, QK_ROPE, 2, dtype=F32) / QK_ROPE))
    ang = jnp.arange(s, dtype=F32)[:, None] * inv[None, :]
    cos, sin = jnp.cos(ang), jnp.sin(ang)
    one = jnp.ones((s, QK_NOPE), F32)
    zero = jnp.zeros((s, HALF_ROPE), F32)
    tail = jnp.zeros((s, HEAD_PAD - QK_NOPE - QK_ROPE), F32)
    zn = jnp.zeros((s, QK_NOPE), F32)
    cos_t = jnp.concatenate([one, cos, cos, tail], axis=1)
    sin_lo = jnp.concatenate([zn, -sin, zero, tail], axis=1)
    sin_hi = jnp.concatenate([zn, zero, sin, tail], axis=1)
    return jnp.stack([cos_t, sin_lo, sin_hi])


def _prep_weights(w_in, lru_w_r, lru_w_i, w_q_b, w_kv_b, w_up):
    o1, o2 = D_RNN, 2 * D_RNN
    o3, o4, o5 = o2 + Q_LORA, o2 + Q_LORA + KV_LORA, o2 + Q_LORA + KV_LORA + QK_ROPE
    o6 = o5 + D_MODEL
    kr = jnp.pad(w_in[:, o4:o5], ((0, 0), (QK_NOPE, HEAD_PAD - QK_NOPE - QK_ROPE)))
    w1 = jnp.concatenate([w_in[:, :o2], w_in[:, o5:o6], w_in[:, o6:], w_in[:, o2:o4], kr],
                         axis=1).astype(BF16)
    eye = jnp.eye(RNN_BLOCKS // N_GATE_GROUPS, dtype=F32)

    def blockdiag(w):
        wgp = w.reshape(2, N_GATE_GROUPS, RNN_BLOCKS // N_GATE_GROUPS, RNN_BLOCK, RNN_BLOCK)
        return jnp.einsum("dgnkj,nm->dgnkmj", wgp, eye).reshape(2, N_GATE_GROUPS, GATE_GROUP, GATE_GROUP)

    wgate = jnp.concatenate([blockdiag(lru_w_r), blockdiag(lru_w_i)], axis=-1).astype(BF16)
    qscale = (QK_NOPE + QK_ROPE) ** -0.5 * math.log2(math.e)
    wq = jnp.pad((w_q_b * qscale).reshape(Q_LORA, N_HEADS, QK_NOPE + QK_ROPE),
                 ((0, 0), (0, 0), (0, HEAD_PAD - QK_NOPE - QK_ROPE)))
    wq = wq.reshape(Q_LORA, N_HEADS * HEAD_PAD).astype(BF16)
    wkv = w_kv_b.reshape(KV_LORA, N_HEADS, QK_NOPE + V_HEAD)
    wk = jnp.pad(wkv[:, :, :QK_NOPE], ((0, 0), (0, 0), (0, HEAD_PAD - QK_NOPE)))
    wk = wk.reshape(KV_LORA, N_HEADS * HEAD_PAD).astype(BF16)
    wvt = wkv[:, :, QK_NOPE:].reshape(KV_LORA, N_HEADS * V_HEAD).T.astype(BF16)
    return w1, wgate, wq, wk, wvt, w_up[:, :D_FF].astype(BF16), w_up[:, D_FF:].astype(BF16)


def _tile(s, pref):
    return min(s, pref)


def _layer(x, mod, p, *, tm_in=512, ts=512, tq=256, tk=2048, tm_merge=512, tm_ffn=512):
    s = x.shape[1]
    xc, gg, sa, sb, q, k, vt = _inproj(x, mod, p["rope_tab"], p["w1"], p["q_norm"], p["wq"], p["kv_norm"],
                                       p["wk"], p["wvt"], p["rnn_conv_w"], p["rnn_conv_b"],
                                       _tile(s, tm_in))
    hf, hb = _lru(xc, p["wgate"], p["lru_b_r"], p["lru_b_i"], p["lru_lam"], _tile(s, ts))
    o = _attn(q, k, vt, _tile(s, tq), _tile(s // 2, tk))
    x1, u2 = _merge(x, hf, hb, gg, sa, sb, o, mod, p["w_rnn_proj"], p["w_mla_proj"], p["w_out"],
                    p["ln1_g"], p["ln1_b"], _tile(s, tm_merge))
    return _ffn(u2, x1, mod, p["w_up_v"], p["w_up_g"], p["ffn_conv_w"], p["ffn_conv_b"], p["w_down"],
                p["ln2_g"], p["ln2_b"], _tile(s, tm_ffn))


def kernel(x_prompt, x_sample, c_prompt, c_sample, w_ada, b_ada, w_in, rnn_conv_w, rnn_conv_b, lru_w_r, lru_b_r, lru_w_i, lru_b_i, lru_lam, w_rnn_proj, q_norm, w_q_b, kv_norm, w_kv_b, w_mla_proj, w_out, ln1_g, ln1_b, w_up, ffn_conv_w, ffn_conv_b, w_down, ln2_g, ln2_b):
    assert w_ada.shape[0] == DEPTH
    l = 0
    nb_p, nb_s = c_prompt.shape[0], c_sample.shape[0]
    rows = -(-(nb_p + nb_s) // SUBLANES) * SUBLANES
    c_all = jnp.concatenate([c_prompt, c_sample, jnp.zeros((rows - nb_p - nb_s, D_MODEL), F32)])
    mod = _ada(c_all, w_ada[l], b_ada[l][None, :]).reshape(rows, 6, D_MODEL)
    mod = jnp.pad(mod, ((0, 0), (0, 2), (0, 0)))
    w1, wgate, wq, wk, wvt, w_up_v, w_up_g = _prep_weights(w_in[l], lru_w_r[l], lru_w_i[l], w_q_b[l],
                                                         w_kv_b[l], w_up[l])
    row = lambda a: a[l][None, :]
    p = {
        "rope_tab": _rope_table(max(x_prompt.shape[1], x_sample.shape[1])),
        "w1": w1, "wgate": wgate, "wq": wq, "wk": wk, "wvt": wvt, "w_up_v": w_up_v, "w_up_g": w_up_g,
        "q_norm": row(q_norm), "kv_norm": row(kv_norm),
        "rnn_conv_w": rnn_conv_w[l], "rnn_conv_b": row(rnn_conv_b),
        "lru_b_r": lru_b_r[l], "lru_b_i": lru_b_i[l], "lru_lam": lru_lam[l],
        "w_rnn_proj": w_rnn_proj[l].astype(BF16), "w_mla_proj": w_mla_proj[l].astype(BF16),
        "w_out": w_out[l].astype(BF16), "ln1_g": row(ln1_g), "ln1_b": row(ln1_b),
        "ffn_conv_w": ffn_conv_w[l], "ffn_conv_b": row(ffn_conv_b),
        "w_down": w_down[l].astype(BF16), "ln2_g": row(ln2_g), "ln2_b": row(ln2_b),
    }
    y_prompt = _layer(x_prompt, mod[:nb_p], p)
    y_sample = _layer(x_sample, mod[nb_p:nb_p + nb_s], p)
    return (y_prompt, y_sample)
```

```python
import functools
import math

import jax
import jax.numpy as jnp
from jax import lax
from jax.experimental import pallas as pl
from jax.experimental.pallas import tpu as pltpu

F32 = jnp.float32
BF16 = jnp.bfloat16

D_MODEL = 1024
D_RNN = 1024
RNN_BLOCKS = 16
RNN_BLOCK = D_RNN // RNN_BLOCKS
RNN_CONV = 4
RNN_CONV_LEFT = 2
LRU_C = 8.0
N_HEADS = 16
QK_NOPE = 64
QK_ROPE = 32
V_HEAD = 64
Q_LORA = 384
KV_LORA = 256
ROPE_BASE = 10000.0
D_FF = 2816
FFN_CONV = 3
LN_EPS = 1e-5
RMS_EPS = 1e-6
DEPTH = 1
DN_ALPHA = (2.0 * DEPTH) ** 0.25

LANES = 128
SUBLANES = 8
BF16_ROWS = 16
MXU_DIM = 256
VMEM_LIMIT = 56 * 1024 * 1024

HEAD_PAD = LANES
HALF_ROPE = QK_ROPE // 2
GATE_GROUP = MXU_DIM
N_GATE_GROUPS = D_RNN // GATE_GROUP
ROW_SUBTILE = MXU_DIM
ATTN_UNROLL = 8
FF_CHUNK = MXU_DIM
N_FF_CHUNKS = D_FF // FF_CHUNK
OFF_QL = 4 * D_MODEL
OFF_KVL = OFF_QL + Q_LORA
OFF_KR = OFF_KVL + KV_LORA
N_IN_PAD = OFF_KR + HEAD_PAD


def _const_spec(shape):
    nd = len(shape)
    return pl.BlockSpec(shape, lambda *_: (0,) * nd, pipeline_mode=pl.Buffered(1))


def _params(n_grid):
    return pltpu.CompilerParams(dimension_semantics=("arbitrary",) * n_grid,
                                vmem_limit_bytes=VMEM_LIMIT)


def _ln_plain(x):
    mu = jnp.mean(x, axis=-1, keepdims=True)
    xc = x - mu
    var = jnp.mean(xc * xc, axis=-1, keepdims=True)
    return xc * lax.rsqrt(var + LN_EPS)


def _rms(x, g):
    return x * lax.rsqrt(jnp.mean(x * x, axis=-1, keepdims=True) + RMS_EPS) * g


def _dot(a, b):
    return jnp.dot(a, b, preferred_element_type=F32)


def _ada_kernel(c_ref, w_ref, b_ref, o_ref):
    c = c_ref[...]
    s = (c * jax.nn.sigmoid(c)).astype(BF16)
    o_ref[...] = _dot(s, w_ref[...].astype(BF16)) + b_ref[...]


def _ada(c_all, w_ada, b_ada, tn=512):
    rows, n = c_all.shape[0], w_ada.shape[1]
    return pl.pallas_call(
        _ada_kernel,
        grid=(n // tn,),
        in_specs=[pl.BlockSpec((rows, D_MODEL), lambda j: (0, 0)),
                  pl.BlockSpec((D_MODEL, tn), lambda j: (0, j)),
                  pl.BlockSpec((1, tn), lambda j: (0, j))],
        out_specs=pl.BlockSpec((rows, tn), lambda j: (0, j)),
        out_shape=jax.ShapeDtypeStruct((rows, n), F32),
        compiler_params=_params(1),
        name="ada",
    )(c_all, w_ada, b_ada)


def _inproj_kernel(x_ref, xp_ref, xn_ref, mod_ref, rope_ref, w1_ref, qn_ref, wq_ref, kvn_ref, wk_ref,
                   wvt_ref, cw_ref, cb_ref, xc_ref, gg_ref, sa_ref, sb_ref, q_ref, k_ref, vt_ref, xs_ref):
    i = pl.program_id(1)
    last = pl.num_programs(1) - 1
    tm = x_ref.shape[0]
    mod = mod_ref[...]
    cw = cw_ref[...]

    def modulated(x):
        return (_ln_plain(x) * (1.0 + mod[1:2]) + mod[0:1]).astype(BF16)

    for n, r0 in enumerate(range(0, tm, ROW_SUBTILE)):
        rows = slice(r0, r0 + ROW_SUBTILE)
        ub = modulated(x_ref[rows, :])

        def proj(lo, hi):
            return _dot(ub, w1_ref[:, lo:hi])

        x_prev = x_ref[r0 - SUBLANES:r0, :] if r0 > 0 else xp_ref[...]
        x_next = x_ref[r0 + ROW_SUBTILE:r0 + ROW_SUBTILE + SUBLANES, :] if r0 + ROW_SUBTILE < tm else xn_ref[...]
        halo = _dot(modulated(jnp.concatenate([x_prev, x_next], axis=0)), w1_ref[:, 0:D_RNN])
        keep_prev = 1.0 if r0 > 0 else (i > 0).astype(F32)
        keep_next = 1.0 if r0 + ROW_SUBTILE < tm else (i < last).astype(F32)
        xs_ref[n, 0:SUBLANES, :] = halo[0:SUBLANES] * keep_prev
        xs_ref[n, SUBLANES + ROW_SUBTILE:, :] = halo[SUBLANES:] * keep_next

        cos_t, sin_lo, sin_hi = rope_ref[0, rows, :], rope_ref[1, rows, :], rope_ref[2, rows, :]

        def rope(t):
            return (t * cos_t + pltpu.roll(t, HEAD_PAD - HALF_ROPE, 1) * sin_lo
                    + pltpu.roll(t, HALF_ROPE, 1) * sin_hi)

        ql = proj(OFF_QL, OFF_KVL)
        kvl = proj(OFF_KVL, OFF_KR)
        kr = proj(OFF_KR, N_IN_PAD)
        xs_ref[n, SUBLANES:SUBLANES + ROW_SUBTILE, :] = proj(0, D_RNN)
        qn = _rms(ql, qn_ref[...]).astype(BF16)
        kvn = _rms(kvl, kvn_ref[...]).astype(BF16)
        kpe = rope(kr)
        gg_ref[rows, :] = jax.nn.gelu(proj(D_MODEL, 2 * D_MODEL)).astype(BF16)
        qf = _dot(qn, wq_ref[...])
        kf = _dot(kvn, wk_ref[...])
        vt_ref[:, rows] = lax.dot_general(wvt_ref[...], kvn, (((1,), (1,)), ((), ())),
                                          preferred_element_type=F32).astype(BF16)
        sa_ref[rows, :] = jax.nn.sigmoid(proj(2 * D_MODEL, 3 * D_MODEL)).astype(BF16)
        xc = cb_ref[...] + cw[0:1] * xs_ref[n, pl.ds(SUBLANES - RNN_CONV_LEFT, ROW_SUBTILE), :]
        for kk in range(1, RNN_CONV):
            xc = xc + cw[kk:kk + 1] * xs_ref[n, pl.ds(SUBLANES - RNN_CONV_LEFT + kk, ROW_SUBTILE), :]
        xc_ref[rows, :] = xc
        for h in range(N_HEADS):
            sl = slice(h * HEAD_PAD, (h + 1) * HEAD_PAD)
            q_ref[rows, sl] = rope(qf[:, sl]).astype(BF16)
            k_ref[rows, sl] = (kf[:, sl] + kpe).astype(BF16)
        sb_ref[rows, :] = jax.nn.sigmoid(proj(3 * D_MODEL, 4 * D_MODEL)).astype(BF16)


def _inproj(x, mod, rope_tab, w1, qn, wq, kvn, wk, wv, cw, cb, tm):
    b, s, _ = x.shape
    hpb = tm // SUBLANES
    nhb = s // SUBLANES
    tok = lambda w: pl.BlockSpec((None, tm, w), lambda bi, i: (bi, i, 0))
    halo = lambda f: pl.BlockSpec((None, SUBLANES, D_MODEL), lambda bi, i: (bi, f(i), 0))
    shp = lambda w, dt: jax.ShapeDtypeStruct((b, s, w), dt)
    return pl.pallas_call(
        _inproj_kernel,
        grid=(b, s // tm),
        in_specs=[tok(D_MODEL),
                  halo(lambda i: jnp.maximum(i * hpb - 1, 0)),
                  halo(lambda i: jnp.minimum((i + 1) * hpb, nhb - 1)),
                  pl.BlockSpec((None, 8, D_MODEL), lambda bi, i: (bi, 0, 0)),
                  pl.BlockSpec((3, tm, HEAD_PAD), lambda bi, i: (0, i, 0)),
                  _const_spec(w1.shape), _const_spec(qn.shape), _const_spec(wq.shape),
                  _const_spec(kvn.shape), _const_spec(wk.shape), _const_spec(wv.shape),
                  _const_spec(cw.shape), _const_spec(cb.shape)],
        scratch_shapes=[pltpu.VMEM((tm // ROW_SUBTILE, ROW_SUBTILE + 2 * SUBLANES, D_RNN), F32)],
        out_specs=[tok(D_RNN), tok(D_RNN), tok(D_MODEL), tok(D_MODEL),
                   tok(N_HEADS * HEAD_PAD), tok(N_HEADS * HEAD_PAD),
                   pl.BlockSpec((None, N_HEADS * V_HEAD, tm), lambda bi, i: (bi, 0, i))],
        out_shape=[shp(D_RNN, F32), shp(D_RNN, BF16), shp(D_MODEL, BF16), shp(D_MODEL, BF16),
                   shp(N_HEADS * HEAD_PAD, BF16), shp(N_HEADS * HEAD_PAD, BF16),
                   jax.ShapeDtypeStruct((b, N_HEADS * V_HEAD, s), BF16)],
        compiler_params=_params(2),
        name="inproj",
    )(x, x, x, mod, rope_tab, w1, qn, wq, kvn, wk, wv, cw, cb)


def _lru_kernel(xf_ref, xb_ref, wg_ref, br_ref, bi_ref, lam_ref,
                hf_ref, hb_ref, a_ref, u_ref, carry_ref, *, ts):
    @pl.when(pl.program_id(1) == 0)
    def _():
        carry_ref[...] = jnp.zeros_like(carry_ref)

    lam = lam_ref[...]
    sp = jnp.maximum(-lam, 0.0) + jnp.log1p(jnp.exp(-jnp.abs(lam)))
    half_c = (-0.5 * LRU_C) * sp
    row = lax.broadcasted_iota(jnp.int32, (SUBLANES, D_RNN), 0)
    n_grp = ts // SUBLANES

    def gates(d, cur_ref):
        xc = cur_ref[...]
        xcb = xc.astype(BF16)
        for g in range(N_GATE_GROUPS):
            sl = slice(g * GATE_GROUP, (g + 1) * GATE_GROUP)
            pre = _dot(xcb[:, sl], wg_ref[d, g])
            tr = jnp.tanh(0.5 * (pre[:, :GATE_GROUP] + br_ref[d:d + 1, sl]))
            ti = jnp.tanh(0.5 * (pre[:, GATE_GROUP:] + bi_ref[d:d + 1, sl]))
            log_a = tr * half_c[d:d + 1, sl] + half_c[d:d + 1, sl]
            a = jnp.exp(log_a)
            a_ref[d, :, sl] = a
            z = -jnp.tanh(log_a) * (a * a + 1.0)
            mult = jnp.where(z > 0.0, z * lax.rsqrt(z), 0.0)
            u_ref[d, :, sl] = mult * (0.5 * ti + 0.5) * xc[:, sl]

    def group(d, out_ref, g, carry):
        off = pl.multiple_of(g * SUBLANES, SUBLANES)
        a8 = a_ref[d, pl.ds(off, SUBLANES), :]
        u8 = u_ref[d, pl.ds(off, SUBLANES), :]
        entry = 0 if d == 0 else SUBLANES - 1
        u8 = u8 + jnp.where(row == entry, a8 * carry, 0.0)
        for sh in (1, 2, 4):
            if d == 0:
                edge, shift = row < sh, sh
            else:
                edge, shift = row >= SUBLANES - sh, SUBLANES - sh
            u8 = a8 * jnp.where(edge, 0.0, pltpu.roll(u8, shift, 0)) + u8
            if sh < 4:
                a8 = a8 * jnp.where(edge, 1.0, pltpu.roll(a8, shift, 0))
        out_ref[pl.ds(off, SUBLANES), :] = u8
        exit_ = SUBLANES - 1 - entry
        return u8[exit_:exit_ + 1, :]

    def scan_body(j, carry):
        return group(0, hf_ref, j, carry[0]), group(1, hb_ref, n_grp - 1 - j, carry[1])

    gates(0, xf_ref)
    gates(1, xb_ref)
    cf, cb = lax.fori_loop(0, n_grp, scan_body, (carry_ref[0:1, :], carry_ref[1:2, :]))
    carry_ref[0:1, :] = cf
    carry_ref[1:2, :] = cb


def _lru(xc, wg, br, bi, lam, ts):
    b, s, _ = xc.shape
    nc = s // ts
    fwd = pl.BlockSpec((None, ts, D_RNN), lambda bi_, c: (bi_, c, 0))
    bwd = pl.BlockSpec((None, ts, D_RNN), lambda bi_, c: (bi_, nc - 1 - c, 0))
    return pl.pallas_call(
        functools.partial(_lru_kernel, ts=ts),
        grid=(b, nc),
        in_specs=[fwd, bwd, _const_spec(wg.shape),
                  _const_spec(br.shape), _const_spec(bi.shape), _const_spec(lam.shape)],
        out_specs=[fwd, bwd],
        out_shape=[jax.ShapeDtypeStruct((b, s, D_RNN), F32)] * 2,
        scratch_shapes=[pltpu.VMEM((2, ts, D_RNN), F32),
                        pltpu.VMEM((2, ts, D_RNN), F32),
                        pltpu.VMEM((SUBLANES, D_RNN), F32)],
        compiler_params=_params(2),
        name="lru",
    )(xc, xc, wg, br, bi, lam)


def _attn_kernel(q_ref, k_ref, vt_ref, o_ref, st_ref, *, tq, tk):
    s_len = k_ref.shape[0]
    n_kv, n_q = s_len // tk, s_len // tq
    total = n_q * n_kv
    unroll = min(ATTN_UNROLL, total)
    assert unroll % 2 == 0 and total % unroll == 0 and n_kv & (n_kv - 1) == 0
    per_tile = max(n_kv // unroll, 1)
    tiles_per_body = max(unroll // n_kv, 1)
    assert per_tile & (per_tile - 1) == 0
    heads = [slice(hh * HEAD_PAD, (hh + 1) * HEAD_PAD) for hh in range(2)]

    def scores(t, hh, slot):
        t = jnp.minimum(t, total - 1)
        i = lax.shift_right_logical(t, n_kv.bit_length() - 1)
        j = jnp.bitwise_and(t, n_kv - 1)
        q = q_ref[pl.ds(pl.multiple_of(i * tq, tq), tq), heads[hh]]
        k = k_ref[pl.ds(pl.multiple_of(j * tk, tk), tk), heads[hh]]
        st = lax.dot_general(k, q, (((1,), (1,)), ((), ())), preferred_element_type=F32)
        st_ref[slot, hh] = st
        return jnp.max(st, axis=0, keepdims=True)

    ones = jnp.ones((BF16_ROWS, MXU_DIM), BF16)

    def softmax_pv(j, hh, slot, m_old, acc, cmax):
        off = j * tk if isinstance(j, int) else pl.multiple_of(j * tk, tk)
        m_new = jnp.maximum(m_old, cmax)
        acc = jnp.exp2(m_old - m_new) * acc
        for t in range(tk // MXU_DIM):
            rows = slice(t * MXU_DIM, (t + 1) * MXU_DIM)
            pt = jnp.exp2(st_ref[slot, hh, rows, :] - m_new).astype(BF16)
            vt = vt_ref[hh * V_HEAD:(hh + 1) * V_HEAD, pl.ds(off + t * MXU_DIM, MXU_DIM)]
            acc = acc + _dot(jnp.concatenate([vt, ones], axis=0), pt)
        return m_new, acc

    def finish(i, carry):
        ot = jnp.concatenate([acc[:V_HEAD] / acc[V_HEAD:V_HEAD + 1] for _, acc, _ in carry], axis=0)
        row0 = i * tq if isinstance(i, int) else pl.multiple_of(i * tq, tq)
        o_ref[pl.ds(row0, tq), :] = ot.T.astype(o_ref.dtype)

    def body(bb, carry):
        jb = jnp.bitwise_and(bb, per_tile - 1)
        for step in range(unroll):
            slot = step % 2
            j = jb * unroll + step if tiles_per_body == 1 else step % n_kv
            new = []
            for hh in range(2):
                m_old, acc, cmax = carry[hh]
                cmax_next = scores(bb * unroll + step + 1, hh, 1 - slot)
                new.append(softmax_pv(j, hh, slot, m_old, acc, cmax) + (cmax_next,))
            carry = tuple(new)
            if tiles_per_body > 1 and (step + 1) % n_kv == 0:
                finish(bb * tiles_per_body + step // n_kv, carry)
                carry = tuple((m0, acc0, cmax) for _, _, cmax in carry)
        if tiles_per_body > 1:
            return carry
        i = lax.shift_right_logical(bb, per_tile.bit_length() - 1)
        if per_tile == 1:
            finish(i, carry)
            return tuple((m0, acc0, cmax) for _, _, cmax in carry)
        tile_done = jb == per_tile - 1
        pl.when(tile_done)(lambda: finish(i, carry))
        return tuple((jnp.where(tile_done, m0, m), jnp.where(tile_done, acc0, acc), cmax)
                     for m, acc, cmax in carry)

    m0 = jnp.full((1, tq), -jnp.inf, F32)
    acc0 = jnp.zeros((V_HEAD + BF16_ROWS, tq), F32)
    lax.fori_loop(0, total // unroll, body, tuple((m0, acc0, scores(0, hh, 0)) for hh in range(2)))


def _attn(q, k, vt, tq, tk):
    b, s, _ = q.shape
    resident = s * (4 * HEAD_PAD + 2 * V_HEAD) * 2
    mode = {} if 2 * resident <= VMEM_LIMIT // 4 else {"pipeline_mode": pl.Buffered(1)}
    return pl.pallas_call(
        functools.partial(_attn_kernel, tq=tq, tk=tk),
        scratch_shapes=[pltpu.VMEM((2, 2, tk, tq), F32)],
        grid=(b, N_HEADS // 2),
        in_specs=[pl.BlockSpec((None, s, 2 * HEAD_PAD), lambda bi, p: (bi, 0, p), **mode),
                  pl.BlockSpec((None, s, 2 * HEAD_PAD), lambda bi, p: (bi, 0, p), **mode),
                  pl.BlockSpec((None, 2 * V_HEAD, s), lambda bi, p: (bi, p, 0), **mode)],
        out_specs=pl.BlockSpec((None, s, 2 * V_HEAD), lambda bi, p: (bi, 0, p)),
        out_shape=jax.ShapeDtypeStruct((b, s, N_HEADS * V_HEAD), BF16),
        compiler_params=_params(2),
        name="attn",
    )(q, k, vt)


def _merge_kernel(x_ref, hf_ref, hb_ref, gg_ref, sa_ref, sb_ref, o_ref, mod_ref,
                  wr_ref, wm_ref, wo_ref, g_ref, b_ref, x1_ref, u2_ref):
    mod = mod_ref[...]
    for r0 in range(0, x_ref.shape[0], ROW_SUBTILE):
        rows = slice(r0, r0 + ROW_SUBTILE)
        yb = _dot(o_ref[rows, :], wm_ref[...])
        ya = _dot((gg_ref[rows, :] * (hf_ref[rows, :] + hb_ref[rows, :])).astype(BF16), wr_ref[...])
        mix = _dot((sa_ref[rows, :] * ya + sb_ref[rows, :] * yb).astype(BF16), wo_ref[...])
        x1 = _ln_plain(DN_ALPHA * x_ref[rows, :] + mod[2:3] * mix) * g_ref[...] + b_ref[...]
        x1_ref[rows, :] = x1
        u2_ref[rows, :] = (_ln_plain(x1) * (1.0 + mod[4:5]) + mod[3:4]).astype(BF16)


def _merge(x, hf, hb, gg, sa, sb, o, mod, wr, wm, wo, g1, b1, tm):
    b, s, _ = x.shape
    tok = pl.BlockSpec((None, tm, D_MODEL), lambda bi, i: (bi, i, 0))
    return pl.pallas_call(
        _merge_kernel,
        grid=(b, s // tm),
        in_specs=[tok] * 7 + [pl.BlockSpec((None, 8, D_MODEL), lambda bi, i: (bi, 0, 0)),
                              _const_spec(wr.shape), _const_spec(wm.shape), _const_spec(wo.shape),
                              _const_spec(g1.shape), _const_spec(b1.shape)],
        out_specs=[tok, tok],
        out_shape=[jax.ShapeDtypeStruct((b, s, D_MODEL), F32),
                   jax.ShapeDtypeStruct((b, s, D_MODEL), BF16)],
        compiler_params=_params(2),
        name="merge",
    )(x, hf, hb, gg, sa, sb, o, mod, wr, wm, wo, g1, b1)


def _ffn_kernel(u_ref, up_ref, un_ref, x1_ref, mod_ref, wv_ref, wg_ref, cw_ref, cb_ref, wd_ref,
                g_ref, b_ref, y_ref, ue_ref, hv_ref, hg_ref, act_ref, *, tm):
    i = pl.program_id(1)
    last = pl.num_programs(1) - 1
    halo = BF16_ROWS
    keep_prev = (i > 0).astype(BF16)
    keep_next = (i < last).astype(BF16)
    ue_ref[0:halo, :] = up_ref[...] * keep_prev
    ue_ref[halo:halo + tm, :] = u_ref[...]
    ue_ref[halo + tm:, :] = un_ref[...] * keep_next
    ue = ue_ref[...]

    def up(j):
        sl = slice(j * FF_CHUNK, (j + 1) * FF_CHUNK)
        hv_ref[j % 2] = _dot(ue, wv_ref[:, sl])
        hg_ref[j % 2] = _dot(ue, wg_ref[:, sl])

    def conv(h_ref, slot, w, bias):
        return (bias + w[0:1] * h_ref[slot, pl.ds(halo - 1, tm), :]
                + w[1:2] * h_ref[slot, pl.ds(halo, tm), :]
                + w[2:3] * h_ref[slot, pl.ds(halo + 1, tm), :])

    up(0)
    for j in range(N_FF_CHUNKS):
        if j + 1 < N_FF_CHUNKS:
            up(j + 1)
        sl = slice(j * FF_CHUNK, (j + 1) * FF_CHUNK)
        slg = slice(D_FF + j * FF_CHUNK, D_FF + (j + 1) * FF_CHUNK)
        cv = conv(hv_ref, j % 2, cw_ref[:, sl], cb_ref[:, sl])
        cg = conv(hg_ref, j % 2, cw_ref[:, slg], cb_ref[:, slg])
        act_ref[:, sl] = (cg * jax.nn.sigmoid(cg) * cv).astype(BF16)
    y = _dot(act_ref[...], wd_ref[...])
    mod = mod_ref[...]
    y_ref[...] = _ln_plain(DN_ALPHA * x1_ref[...] + mod[5:6] * y) * g_ref[...] + b_ref[...]


def _ffn(u2, x1, mod, wv, wg, cw, cb, wd, g2, b2, tm):
    b, s, _ = x1.shape
    hpb = tm // BF16_ROWS
    nhb = s // BF16_ROWS
    tok = pl.BlockSpec((None, tm, D_MODEL), lambda bi, i: (bi, i, 0))
    return pl.pallas_call(
        functools.partial(_ffn_kernel, tm=tm),
        grid=(b, s // tm),
        in_specs=[tok,
                  pl.BlockSpec((None, BF16_ROWS, D_MODEL),
                               lambda bi, i: (bi, jnp.maximum(i * hpb - 1, 0), 0)),
                  pl.BlockSpec((None, BF16_ROWS, D_MODEL),
                               lambda bi, i: (bi, jnp.minimum((i + 1) * hpb, nhb - 1), 0)),
                  tok,
                  pl.BlockSpec((None, 8, D_MODEL), lambda bi, i: (bi, 0, 0)),
                  _const_spec(wv.shape), _const_spec(wg.shape), _const_spec(cw.shape),
                  _const_spec(cb.shape), _const_spec(wd.shape), _const_spec(g2.shape),
                  _const_spec(b2.shape)],
        out_specs=tok,
        out_shape=jax.ShapeDtypeStruct((b, s, D_MODEL), F32),
        scratch_shapes=[pltpu.VMEM((tm + 2 * BF16_ROWS, D_MODEL), BF16),
                        pltpu.VMEM((2, tm + 2 * BF16_ROWS, FF_CHUNK), F32),
                        pltpu.VMEM((2, tm + 2 * BF16_ROWS, FF_CHUNK), F32),
                        pltpu.VMEM((tm, D_FF), BF16)],
        compiler_params=_params(2),
        name="ffn",
    )(u2, u2, u2, x1, mod, wv, wg, cw, cb, wd, g2, b2)


def _rope_table(s):
    inv = 1.0 / (ROPE_BASE ** (jnp.arange(0, QK_ROPE, 2, dtype=F32) / QK_ROPE))
    tail = HEAD_PAD - QK_NOPE - QK_ROPE
    inv_lane = jnp.concatenate([jnp.zeros((QK_NOPE,), F32), inv, inv, jnp.zeros((tail,), F32)])
    lo_lane = jnp.concatenate([jnp.zeros((QK_NOPE,), F32), jnp.ones((HALF_ROPE,), F32),
                               jnp.zeros((HALF_ROPE + tail,), F32)])
    hi_lane = jnp.concatenate([jnp.zeros((QK_NOPE + HALF_ROPE,), F32), jnp.ones((HALF_ROPE,), F32),
                               jnp.zeros((tail,), F32)])
    ang = jnp.arange(s, dtype=F32)[:, None] * inv_lane[None, :]
    sin = jnp.sin(ang)
    return jnp.stack([jnp.cos(ang), -sin * lo_lane[None, :], sin * hi_lane[None, :]])


def _prep_weights(w_in, lru_w_r, lru_w_i, w_q_b, w_kv_b, w_up):
    o1, o2 = D_RNN, 2 * D_RNN
    o3, o4, o5 = o2 + Q_LORA, o2 + Q_LORA + KV_LORA, o2 + Q_LORA + KV_LORA + QK_ROPE
    o6 = o5 + D_MODEL
    kr = jnp.pad(w_in[:, o4:o5], ((0, 0), (QK_NOPE, HEAD_PAD - QK_NOPE - QK_ROPE)))
    w1 = jnp.concatenate([w_in[:, :o2], w_in[:, o5:o6], w_in[:, o6:], w_in[:, o2:o4], kr],
                         axis=1).astype(BF16)
    eye = jnp.eye(RNN_BLOCKS // N_GATE_GROUPS, dtype=F32)

    def blockdiag(w):
        wgp = w.reshape(2, N_GATE_GROUPS, RNN_BLOCKS // N_GATE_GROUPS, RNN_BLOCK, RNN_BLOCK)
        return jnp.einsum("dgnkj,nm->dgnkmj", wgp, eye).reshape(2, N_GATE_GROUPS, GATE_GROUP, GATE_GROUP)

    wgate = jnp.concatenate([blockdiag(lru_w_r), blockdiag(lru_w_i)], axis=-1).astype(BF16)
    qscale = (QK_NOPE + QK_ROPE) ** -0.5 * math.log2(math.e)
    wq = jnp.pad((w_q_b * qscale).reshape(Q_LORA, N_HEADS, QK_NOPE + QK_ROPE),
                 ((0, 0), (0, 0), (0, HEAD_PAD - QK_NOPE - QK_ROPE)))
    wq = wq.reshape(Q_LORA, N_HEADS * HEAD_PAD).astype(BF16)
    wkv = w_kv_b.reshape(KV_LORA, N_HEADS, QK_NOPE + V_HEAD)
    wk = jnp.pad(wkv[:, :, :QK_NOPE], ((0, 0), (0, 0), (0, HEAD_PAD - QK_NOPE)))
    wk = wk.reshape(KV_LORA, N_HEADS * HEAD_PAD).astype(BF16)
    wvt = wkv[:, :, QK_NOPE:].reshape(KV_LORA, N_HEADS * V_HEAD).T.astype(BF16)
    return w1, wgate, wq, wk, wvt, w_up[:, :D_FF].astype(BF16), w_up[:, D_FF:].astype(BF16)


def _tile(s, pref):
    return min(s, pref)


def _layer(x, mod, p, *, tm_in=512, ts=512, tq=256, tk=2048, tm_merge=512, tm_ffn=512):
    s = x.shape[1]
    xc, gg, sa, sb, q, k, vt = _inproj(x, mod, p["rope_tab"], p["w1"], p["q_norm"], p["wq"], p["kv_norm"],
                                       p["wk"], p["wvt"], p["rnn_conv_w"], p["rnn_conv_b"],
                                       _tile(s, tm_in))
    hf, hb = _lru(xc, p["wgate"], p["lru_b_r"], p["lru_b_i"], p["lru_lam"], _tile(s, ts))
    o = _attn(q, k, vt, _tile(s, tq), _tile(s // 2, tk))
    x1, u2 = _merge(x, hf, hb, gg, sa, sb, o, mod, p["w_rnn_proj"], p["w_mla_proj"], p["w_out"],
                    p["ln1_g"], p["ln1_b"], _tile(s, tm_merge))
    return _ffn(u2, x1, mod, p["w_up_v"], p["w_up_g"], p["ffn_conv_w"], p["ffn_conv_b"], p["w_down"],
                p["ln2_g"], p["ln2_b"], _tile(s, tm_ffn))


def kernel(x_prompt, x_sample, c_prompt, c_sample, w_ada, b_ada, w_in, rnn_conv_w, rnn_conv_b, lru_w_r, lru_b_r, lru_w_i, lru_b_i, lru_lam, w_rnn_proj, q_norm, w_q_b, kv_norm, w_kv_b, w_mla_proj, w_out, ln1_g, ln1_b, w_up, ffn_conv_w, ffn_conv_b, w_down, ln2_g, ln2_b):
    assert w_ada.shape[0] == DEPTH
    l = 0
    nb_p, nb_s = c_prompt.shape[0], c_sample.shape[0]
    rows = -(-(nb_p + nb_s) // SUBLANES) * SUBLANES
    c_all = jnp.concatenate([c_prompt, c_sample, jnp.zeros((rows - nb_p - nb_s, D_MODEL), F32)])
    mod = _ada(c_all, w_ada[l], b_ada[l][None, :]).reshape(rows, 6, D_MODEL)
    mod = jnp.pad(mod, ((0, 0), (0, 2), (0, 0)))
    w1, wgate, wq, wk, wvt, w_up_v, w_up_g = _prep_weights(w_in[l], lru_w_r[l], lru_w_i[l], w_q_b[l],
                                                         w_kv_b[l], w_up[l])
    row = lambda a: a[l][None, :]
    p = {
        "rope_tab": _rope_table(max(x_prompt.shape[1], x_sample.shape[1])),
        "w1": w1, "wgate": wgate, "wq": wq, "wk": wk, "wvt": wvt, "w_up_v": w_up_v, "w_up_g": w_up_g,
        "q_norm": row(q_norm), "kv_norm": row(kv_norm),
        "rnn_conv_w": rnn_conv_w[l], "rnn_conv_b": row(rnn_conv_b),
        "lru_b_r": lru_b_r[l], "lru_b_i": lru_b_i[l], "lru_lam": lru_lam[l],
        "w_rnn_proj": w_rnn_proj[l].astype(BF16), "w_mla_proj": w_mla_proj[l].astype(BF16),
        "w_out": w_out[l].astype(BF16), "ln1_g": row(ln1_g), "ln1_b": row(ln1_b),
        "ffn_conv_w": ffn_conv_w[l], "ffn_conv_b": row(ffn_conv_b),
        "w_down": w_down[l].astype(BF16), "ln2_g": row(ln2_g), "ln2_b": row(ln2_b),
    }
    y_prompt = _layer(x_prompt, mod[:nb_p], p)
    y_sample = _layer(x_sample, mod[nb_p:nb_p + nb_s], p)
    return (y_prompt, y_sample)
```
